```python
import math, functools
import jax, jax.numpy as jnp
from jax import lax
import numpy as np

D_MODEL = 1024
BATCH = 4
SEQ = 4096
DEPTH = 2
DEC_BATCH = 128
DEC_SEQ = 8
PAST_LEN = 8192
PAGE_SIZE = 128

BRANCH_WIDTH = D_MODEL // 2
N_BRANCH = 4
MLA_HEADS = 8
MLA_NOPE = 64
MLA_ROPE = 32
MLA_V_DIM = BRANCH_WIDTH // MLA_HEADS
Q_LORA = 3 * D_MODEL // 8
KV_LORA = D_MODEL // 4
MLA_SCALE = (MLA_NOPE + MLA_ROPE) ** -0.5
ROPE_BASE = 10000.0
S5_GROUP = 16
S5_GROUPS = BRANCH_WIDTH // S5_GROUP
S5_STATE = 64
DT_MIN = 1e-3
DT_MAX = 1e-1
SB_HEADS = 8
SB_HEAD_DIM = BRANCH_WIDTH // SB_HEADS
SB_SCALE = SB_HEAD_DIM ** -0.5
N_MEM = 256
X_HEADS = 4
X_HEAD_DIM = BRANCH_WIDTH // X_HEADS
X_SCALE = X_HEAD_DIM ** -0.5
Q_BLOCK = 128
RMS_EPS = 1e-6
IN_WIDTHS = (Q_LORA, KV_LORA, MLA_ROPE) + (BRANCH_WIDTH,) * 9 + (N_BRANCH * D_MODEL,)
N_IN = sum(IN_WIDTHS)

kernel_name = 'gated_parallel_mla_s5_stickbreak_memory_decoder_step'


def _in_splits():
    return [int(v) for v in np.cumsum(IN_WIDTHS)[:-1]]


def _rmsnorm(x, g):
    xf = x.astype(jnp.float32)
    xf = xf * lax.rsqrt(jnp.mean(xf * xf, axis=-1, keepdims=True) + RMS_EPS)
    return (xf * g.astype(jnp.float32)).astype(x.dtype)


def _rope(x, pos):
    half = MLA_ROPE // 2
    freq = ROPE_BASE ** (-jnp.arange(half, dtype=jnp.float32) / half)
    ang = pos.astype(jnp.float32)[:, None] * freq
    ang = ang.reshape(ang.shape[:1] + (1,) * (x.ndim - 3) + (half,))
    cos, sin = jnp.cos(ang).astype(x.dtype), jnp.sin(ang).astype(x.dtype)
    x1, x2 = x[..., :half], x[..., half:]
    return jnp.concatenate([x1 * cos - x2 * sin, x1 * sin + x2 * cos], axis=-1)


def _query_blocks(fn, qs, q_pos):
    t = q_pos.shape[0]
    qb = min(Q_BLOCK, t)
    nb = -(-t // qb)
    pad = nb * qb - t

    def split(a):
        a = jnp.pad(a, [(0, 0), (0, pad)] + [(0, 0)] * (a.ndim - 2))
        return jnp.moveaxis(a.reshape(a.shape[0], nb, qb, *a.shape[2:]), 1, 0)

    pos_b = jnp.pad(q_pos, (0, pad)).reshape(nb, qb)
    out = lax.map(lambda args: fn(*args), tuple(split(a) for a in qs) + (pos_b,))
    out = jnp.moveaxis(out, 0, 1)
    return out.reshape(out.shape[0], nb * qb, *out.shape[3:])[:, :t]


def _mla_core(q_lat, q_pe, q_pos, segs):
    s = jnp.concatenate([jnp.einsum('bqhl,bkl->bhqk', q_lat, c) + jnp.einsum('bqhr,bkr->bhqk', q_pe, r)
                         for c, r, _ in segs], axis=-1).astype(jnp.float32) * MLA_SCALE
    k_pos = jnp.concatenate([p for _, _, p in segs])
    s = jnp.where(k_pos[None, :] <= q_pos[:, None], s, -jnp.inf)
    w = jax.nn.softmax(s, axis=-1).astype(q_lat.dtype)
    parts, start = [], 0
    for c, _, p in segs:
        n = p.shape[0]
        parts.append(jnp.einsum('bhqk,bkl->bqhl', w[..., start:start + n], c))
        start += n
    return functools.reduce(jnp.add, parts)


def _sb_core(q, q_pos, segs):
    z = jnp.concatenate([jnp.einsum('bqhd,bkhd->bhqk', q, k) for k, _, _ in segs],
                        axis=-1).astype(jnp.float32) * SB_SCALE
    k_pos = jnp.concatenate([p for _, _, p in segs])
    mask = k_pos[None, :] < q_pos[:, None]
    log_keep = jnp.where(mask, jax.nn.log_sigmoid(-z), 0.0)
    after = lax.cumsum(log_keep, axis=z.ndim - 1, reverse=True) - log_keep
    w = jnp.exp(jnp.where(mask, jax.nn.log_sigmoid(z) + after, -jnp.inf)).astype(q.dtype)
    parts, start = [], 0
    for _, v, p in segs:
        n = p.shape[0]
        parts.append(jnp.einsum('bhqk,bkhd->bqhd', w[..., start:start + n], v))
        start += n
    return functools.reduce(jnp.add, parts)


def _complex_affine_combine(e1, e2):
    ar1, ai1, br1, bi1 = e1
    ar2, ai2, br2, bi2 = e2
    return (ar1 * ar2 - ai1 * ai2, ar1 * ai2 + ai1 * ar2,
            ar2 * br1 - ai2 * bi1 + br2, ar2 * bi1 + ai2 * br1 + bi2)


def _s5_scan(u, h0_re, h0_im, a_re, a_im, log_dt, b_re, b_im, c_re, c_im, d_skip):
    f32 = jnp.float32
    bsz, t, _ = u.shape
    ug = u.astype(f32).reshape(bsz, t, S5_GROUPS, S5_GROUP)
    a_re, a_im = a_re.astype(f32), a_im.astype(f32)
    dt = jnp.exp(log_dt.astype(f32))[:, None]
    mag = jnp.exp(a_re * dt)
    lb_re, lb_im = mag * jnp.cos(a_im * dt), mag * jnp.sin(a_im * dt)
    den = a_re * a_re + a_im * a_im
    n_re = lb_re - 1.0
    f_re = (n_re * a_re + lb_im * a_im) / den
    f_im = (lb_im * a_re - n_re * a_im) / den
    b_re, b_im = b_re.astype(f32), b_im.astype(f32)
    bb_re = f_re[..., None] * b_re - f_im[..., None] * b_im
    bb_im = f_re[..., None] * b_im + f_im[..., None] * b_re
    bu_re = jnp.einsum('btgc,gpc->btgp', ug, bb_re)
    bu_im = jnp.einsum('btgc,gpc->btgp', ug, bb_im)
    h0_re, h0_im = h0_re.astype(f32), h0_im.astype(f32)
    bu_re = bu_re.at[:, 0].add(lb_re * h0_re - lb_im * h0_im)
    bu_im = bu_im.at[:, 0].add(lb_re * h0_im + lb_im * h0_re)
    ar = jnp.broadcast_to(lb_re, bu_re.shape)
    ai = jnp.broadcast_to(lb_im, bu_im.shape)
    _, _, h_re, h_im = lax.associative_scan(_complex_affine_combine, (ar, ai, bu_re, bu_im), axis=1)
    y = (jnp.einsum('btgp,gcp->btgc', h_re, c_re.astype(f32))
         - jnp.einsum('btgp,gcp->btgc', h_im, c_im.astype(f32))
         + d_skip.astype(f32).reshape(S5_GROUPS, S5_GROUP) * ug)
    return y.reshape(bsz, t, BRANCH_WIDTH).astype(u.dtype), h_re[:, -1], h_im[:, -1]


def _s5_glu(y, w_glu, b_glu):
    z = jax.nn.gelu(y)
    a, b = jnp.split(z @ w_glu + b_glu, 2, axis=-1)
    return a * jax.nn.sigmoid(b)


def _cross_attend(q, k, v):
    s = jnp.einsum('bthd,bmhd->bhtm', q, k).astype(jnp.float32) * X_SCALE
    p = jax.nn.softmax(s, axis=-1).astype(v.dtype)
    return jnp.einsum('bhtm,bmhd->bthd', p, v)


def _memory_kv(mem, g_mem, w_mem_kv):
    k, v = jnp.split(_rmsnorm(mem, g_mem) @ w_mem_kv, 2, axis=-1)
    shape = mem.shape[:2] + (X_HEADS, X_HEAD_DIM)
    return k.reshape(shape), v.reshape(shape)


def _layer(x, pos, mem_k, mem_v, s5_h0, past, lw):
    bsz, t, _ = x.shape
    h = _rmsnorm(x, lw['g_norm'])
    (cq, ckv, kpe, g_mla, u_s5, g_s5, q_sb, k_sb, v_sb, g_sb, q_x, g_x, g_merge) = jnp.split(
        h @ lw['w_in'], _in_splits(), axis=-1)
    q = (_rmsnorm(cq, lw['g_q_lat']) @ lw['w_uq']).reshape(bsz, t, MLA_HEADS, MLA_NOPE + MLA_ROPE)
    q_lat = jnp.einsum('bthn,lhn->bthl', q[..., :MLA_NOPE], lw['w_uk'])
    q_pe = _rope(q[..., MLA_NOPE:], pos)
    ckv = _rmsnorm(ckv, lw['g_kv_lat'])
    kpe = _rope(kpe, pos)
    sb_shape = (bsz, t, SB_HEADS, SB_HEAD_DIM)
    sb_q, sb_k, sb_v = q_sb.reshape(sb_shape), k_sb.reshape(sb_shape), v_sb.reshape(sb_shape)
    mla_segs = [(ckv, kpe, pos)]
    sb_segs = [(sb_k, sb_v, pos)]
    if past is not None:
        mla_segs = [past['mla']] + mla_segs
        sb_segs = [past['sb']] + sb_segs
    o_lat = _query_blocks(lambda ql, qp, qpos: _mla_core(ql, qp, qpos, mla_segs), (q_lat, q_pe), pos)
    o_mla = jnp.einsum('bthl,lhv->bthv', o_lat, lw['w_uv']).reshape(bsz, t, BRANCH_WIDTH)
    o_sb = _query_blocks(lambda qq, qpos: _sb_core(qq, qpos, sb_segs), (sb_q,), pos).reshape(bsz, t, BRANCH_WIDTH)
    y_s5, h_re, h_im = _s5_scan(u_s5, s5_h0[0], s5_h0[1], lw['s5_a_re'], lw['s5_a_im'], lw['s5_log_dt'],
                                lw['s5_b_re'], lw['s5_b_im'], lw['s5_c_re'], lw['s5_c_im'], lw['s5_d'])
    o_s5 = _s5_glu(y_s5, lw['w_glu'], lw['b_glu'])
    o_x = _cross_attend(q_x.reshape(bsz, t, X_HEADS, X_HEAD_DIM), mem_k, mem_v).reshape(bsz, t, BRANCH_WIDTH)
    br = jnp.stack([o_mla * jax.nn.silu(g_mla), o_s5 * jax.nn.silu(g_s5),
                    o_sb * jax.nn.silu(g_sb), o_x * jax.nn.silu(g_x)], axis=2)
    proj = jnp.einsum('btnc,ncd->btnd', br, lw['w_branch'])
    gate = jax.nn.sigmoid(g_merge.reshape(bsz, t, N_BRANCH, D_MODEL))
    x = x + jnp.einsum('btd,de->bte', jnp.sum(gate * proj, axis=2), lw['w_out'])
    st_dtype = s5_h0[0].dtype
    return x, (ckv, kpe, sb_k, sb_v, h_re.astype(st_dtype), h_im.astype(st_dtype))


def setup_inputs(seed: int = 0) -> dict:
    key = jax.random.key(seed)
    ks = jax.random.split(key, 48)
    cnt = [0]
    f32 = jnp.float32

    def nxt():
        cnt[0] += 1
        return ks[cnt[0] - 1]

    def nrm(shape, scale):
        return jax.random.normal(nxt(), shape, f32) * scale

    def gain(shape):
        return 1.0 + nrm(shape, 0.02)

    n_pages = PAST_LEN // PAGE_SIZE
    n_used = DEC_BATCH * n_pages
    n_pool = n_used + n_used // 4
    g, p = S5_GROUPS, S5_STATE
    page_table = jax.random.permutation(nxt(), n_pool)[:n_used].reshape(DEC_BATCH, n_pages).astype(jnp.int32)
    return {
        'x_prompt': nrm((BATCH, SEQ, D_MODEL), 1.0),
        'x_sample': nrm((DEC_BATCH, DEC_SEQ, D_MODEL), 1.0),
        'mem_prompt': nrm((BATCH, N_MEM, D_MODEL), 1.0),
        'cache_mla_kv': nrm((DEPTH, n_pool, PAGE_SIZE, KV_LORA), 1.0),
        'cache_mla_pe': nrm((DEPTH, n_pool, PAGE_SIZE, MLA_ROPE), 1.0),
        'cache_sb_k': nrm((DEPTH, n_pool, PAGE_SIZE, SB_HEADS, SB_HEAD_DIM), 1.0),
        'cache_sb_v': nrm((DEPTH, n_pool, PAGE_SIZE, SB_HEADS, SB_HEAD_DIM), 1.0),
        'state_s5_re': nrm((DEPTH, DEC_BATCH, g, p), 0.1),
        'state_s5_im': nrm((DEPTH, DEC_BATCH, g, p), 0.1),
        'cache_mem_k': nrm((DEPTH, DEC_BATCH, N_MEM, X_HEADS, X_HEAD_DIM), 1.0),
        'cache_mem_v': nrm((DEPTH, DEC_BATCH, N_MEM, X_HEADS, X_HEAD_DIM), 1.0),
        'page_table': page_table,
        'g_norm': gain((DEPTH, D_MODEL)),
        'w_in': nrm((DEPTH, D_MODEL, N_IN), D_MODEL ** -0.5),
        'g_q_lat': gain((DEPTH, Q_LORA)),
        'g_kv_lat': gain((DEPTH, KV_LORA)),
        'w_uq': nrm((DEPTH, Q_LORA, MLA_HEADS * (MLA_NOPE + MLA_ROPE)), Q_LORA ** -0.5),
        'w_uk': nrm((DEPTH, KV_LORA, MLA_HEADS, MLA_NOPE), KV_LORA ** -0.5),
        'w_uv': nrm((DEPTH, KV_LORA, MLA_HEADS, MLA_V_DIM), KV_LORA ** -0.5),
        's5_a_re': -0.5 + nrm((DEPTH, g, p), 0.02),
        's5_a_im': jnp.pi * jnp.arange(p, dtype=f32) + nrm((DEPTH, g, p), 0.02),
        's5_log_dt': jax.random.uniform(nxt(), (DEPTH, g), f32, math.log(DT_MIN), math.log(DT_MAX)),
        's5_b_re': nrm((DEPTH, g, p, S5_GROUP), (2 * S5_GROUP) ** -0.5),
        's5_b_im': nrm((DEPTH, g, p, S5_GROUP), (2 * S5_GROUP) ** -0.5),
        's5_c_re': nrm((DEPTH, g, S5_GROUP, p), (2 * S5_STATE) ** -0.5),
        's5_c_im': nrm((DEPTH, g, S5_GROUP, p), (2 * S5_STATE) ** -0.5),
        's5_d': nrm((DEPTH, BRANCH_WIDTH), 1.0),
        'w_glu': nrm((DEPTH, BRANCH_WIDTH, 2 * BRANCH_WIDTH), BRANCH_WIDTH ** -0.5),
        'b_glu': nrm((DEPTH, 2 * BRANCH_WIDTH), 0.01),
        'g_mem': gain((DEPTH, D_MODEL)),
        'w_mem_kv': nrm((DEPTH, D_MODEL, 2 * BRANCH_WIDTH), D_MODEL ** -0.5),
        'w_branch': nrm((DEPTH, N_BRANCH, BRANCH_WIDTH, D_MODEL), BRANCH_WIDTH ** -0.5),
        'w_out': nrm((DEPTH, D_MODEL, D_MODEL), D_MODEL ** -0.5),
        'g_final': gain((D_MODEL,)),
    }


def reference(x_prompt, x_sample, mem_prompt, cache_mla_kv, cache_mla_pe, cache_sb_k, cache_sb_v,
              state_s5_re, state_s5_im, cache_mem_k, cache_mem_v, page_table,
              g_norm, w_in, g_q_lat, g_kv_lat, w_uq, w_uk, w_uv,
              s5_a_re, s5_a_im, s5_log_dt, s5_b_re, s5_b_im, s5_c_re, s5_c_im, s5_d, w_glu, b_glu,
              g_mem, w_mem_kv, w_branch, w_out, g_final):
    bd = x_sample.shape[0]
    past_len = page_table.shape[1] * PAGE_SIZE
    pos_p = jnp.arange(x_prompt.shape[1], dtype=jnp.int32)
    pos_s = past_len + jnp.arange(x_sample.shape[1], dtype=jnp.int32)
    pos_past = jnp.arange(past_len, dtype=jnp.int32)
    h0 = jnp.zeros((x_prompt.shape[0], S5_GROUPS, S5_STATE), jnp.float32)
    xp, xs = x_prompt, x_sample
    new_p = [[] for _ in range(8)]
    new_s = [[] for _ in range(6)]
    for l in range(DEPTH):
        lw = dict(g_norm=g_norm[l], w_in=w_in[l], g_q_lat=g_q_lat[l], g_kv_lat=g_kv_lat[l], w_uq=w_uq[l],
                  w_uk=w_uk[l], w_uv=w_uv[l], s5_a_re=s5_a_re[l], s5_a_im=s5_a_im[l], s5_log_dt=s5_log_dt[l],
                  s5_b_re=s5_b_re[l], s5_b_im=s5_b_im[l], s5_c_re=s5_c_re[l], s5_c_im=s5_c_im[l], s5_d=s5_d[l],
                  w_glu=w_glu[l], b_glu=b_glu[l], w_branch=w_branch[l], w_out=w_out[l])
        mk, mv = _memory_kv(mem_prompt, g_mem[l], w_mem_kv[l])
        xp, st_p = _layer(xp, pos_p, mk, mv, (h0, h0), None, lw)
        past = dict(
            mla=(cache_mla_kv[l, page_table].reshape(bd, past_len, KV_LORA),
                 cache_mla_pe[l, page_table].reshape(bd, past_len, MLA_ROPE), pos_past),
            sb=(cache_sb_k[l, page_table].reshape(bd, past_len, SB_HEADS, SB_HEAD_DIM),
                cache_sb_v[l, page_table].reshape(bd, past_len, SB_HEADS, SB_HEAD_DIM), pos_past))
        xs, st_s = _layer(xs, pos_s, cache_mem_k[l], cache_mem_v[l], (state_s5_re[l], state_s5_im[l]), past, lw)
        for lst, a in zip(new_p, st_p + (mk, mv)):
            lst.append(a)
        for lst, a in zip(new_s, st_s):
            lst.append(a)
    y_prompt = _rmsnorm(xp, g_final)
    y_sample = _rmsnorm(xs, g_final)
    p_kv, p_pe, p_k, p_v, p_re, p_im, p_mk, p_mv = [jnp.stack(a) for a in new_p]
    s_kv, s_pe, s_k, s_v, s_re, s_im = [jnp.stack(a) for a in new_s]
    return (y_prompt, y_sample, p_kv, p_pe, p_k, p_v, p_re, p_im, p_mk, p_mv,
            s_kv, s_pe, s_k, s_v, s_re, s_im)
```

```python
import functools
import math

import jax
import jax.numpy as jnp
from jax import lax
from jax.experimental import pallas as pl
from jax.experimental.pallas import tpu as pltpu

F32 = jnp.float32
BF16 = jnp.bfloat16

D_MODEL = 1024
BRANCH_WIDTH = D_MODEL // 2
N_BRANCH = 4
MLA_HEADS = 8
MLA_NOPE = 64
MLA_ROPE = 32
MLA_V_DIM = BRANCH_WIDTH // MLA_HEADS
Q_LORA = 3 * D_MODEL // 8
KV_LORA = D_MODEL // 4
MLA_SCALE = (MLA_NOPE + MLA_ROPE) ** -0.5
ROPE_BASE = 10000.0
S5_GROUP = 16
S5_GROUPS = BRANCH_WIDTH // S5_GROUP
S5_STATE = 64
S5_WIDTH = S5_GROUPS * S5_STATE
SB_HEADS = 8
SB_HEAD_DIM = BRANCH_WIDTH // SB_HEADS
SB_SCALE = SB_HEAD_DIM ** -0.5
X_HEADS = 4
X_HEAD_DIM = BRANCH_WIDTH // X_HEADS
X_SCALE = X_HEAD_DIM ** -0.5
RMS_EPS = 1e-6
PAGE_SIZE = 128

LANES = 128
ROPE_PAD = LANES
VMEM_LIMIT = 56 * 1024 * 1024

P2_GMERGE_W = N_BRANCH * D_MODEL
_P2_BASE = P2_GMERGE_W // BRANCH_WIDTH
P2_G_MLA, P2_U_S5, P2_G_S5, P2_Q_SB, P2_K_SB, P2_V_SB, P2_G_SB, P2_Q_X, P2_G_X = (
    _P2_BASE + i for i in range(9))
P2_WIDTH = P2_GMERGE_W + 9 * BRANCH_WIDTH


def _params(*sem):
    return pltpu.CompilerParams(dimension_semantics=sem, vmem_limit_bytes=VMEM_LIMIT)


def _rms(x, g):
    return x * lax.rsqrt(jnp.mean(x * x, axis=-1, keepdims=True) + RMS_EPS) * g


def _dot(a, b):
    return jnp.dot(a, b, preferred_element_type=F32)


def _dot_nt(a, b):
    return lax.dot_general(a, b, (((1,), (1,)), ((), ())), preferred_element_type=F32)


def _split_bf16(x):
    hi = x.astype(BF16)
    lo = (x - hi.astype(F32)).astype(BF16)
    return hi, lo


def _neg_softplus(z):
    return -(jnp.maximum(z, 0.0) + jnp.log1p(jnp.exp(-jnp.abs(z))))


def _norm_mm_body(x_ref, g_ref, w_ref, o_ref):
    h = _rms(x_ref[...], g_ref[...]).astype(BF16)
    o_ref[...] = _dot(h, w_ref[...])


def norm_matmul(x, g, w, tn):
    m, d = x.shape
    n = w.shape[1]
    tm = min(512, m)
    return pl.pallas_call(
        _norm_mm_body,
        grid=(n // tn, m // tm),
        in_specs=[pl.BlockSpec((tm, d), lambda j, i: (i, 0)),
                  pl.BlockSpec((1, d), lambda j, i: (0, 0)),
                  pl.BlockSpec((d, tn), lambda j, i: (0, j))],
        out_specs=pl.BlockSpec((tm, tn), lambda j, i: (i, j)),
        out_shape=jax.ShapeDtypeStruct((m, n), F32),
        compiler_params=_params("parallel", "parallel"),
        name="norm_matmul",
    )(x, g.reshape(1, d), w)


def _rope_padded(x, cos_t, sin_t):
    n = x.shape[-1]
    half = MLA_ROPE // 2
    lane = lax.broadcasted_iota(jnp.int32, x.shape, 1) % ROPE_PAD
    swapped = jnp.where(lane < half, pltpu.roll(x, n - half, 1), pltpu.roll(x, half, 1))
    reps = n // ROPE_PAD
    if reps > 1:
        cos_t = jnp.concatenate([cos_t] * reps, axis=1)
        sin_t = jnp.concatenate([sin_t] * reps, axis=1)
    return x * cos_t + swapped * sin_t


def _mla_prep_body(x_ref, gn_ref, w1_ref, gq_ref, gkv_ref, wqn_ref, wqp_ref, wuk_ref, cos_ref, sin_ref,
                   qlat_ref, qpe_ref, ckv_ref, kpe_ref):
    h = _rms(x_ref[...], gn_ref[...]).astype(BF16)
    c = _dot(h, w1_ref[...])
    cq = c[:, :Q_LORA]
    ckv_raw = c[:, Q_LORA:Q_LORA + KV_LORA]
    kpe_raw = c[:, Q_LORA + KV_LORA:]
    cos_t = cos_ref[...]
    sin_t = sin_ref[...]
    ckv_ref[...] = _rms(ckv_raw, gkv_ref[...])
    kpe_ref[...] = _rope_padded(kpe_raw, cos_t, sin_t)
    qn = _rms(cq, gq_ref[...]).astype(BF16)
    q_nope = _dot(qn, wqn_ref[...])
    q_pe = _rope_padded(_dot(qn, wqp_ref[...]), cos_t, sin_t)
    q_lat = _dot(q_nope.astype(BF16), wuk_ref[...])
    for hh in range(MLA_HEADS):
        qlat_ref[hh] = q_lat[:, hh * KV_LORA:(hh + 1) * KV_LORA]
        qpe_ref[hh] = q_pe[:, hh * ROPE_PAD:(hh + 1) * ROPE_PAD]


def mla_prep(x, lw, cos_t, sin_t):
    m, d = x.shape
    tm = min(256, m)
    row = lambda i: (i, 0)
    const = lambda i: (0, 0)
    w1 = lw["w1"]
    return pl.pallas_call(
        _mla_prep_body,
        grid=(m // tm,),
        in_specs=[pl.BlockSpec((tm, d), row),
                  pl.BlockSpec((1, d), const),
                  pl.BlockSpec(w1.shape, const),
                  pl.BlockSpec((1, Q_LORA), const),
                  pl.BlockSpec((1, KV_LORA), const),
                  pl.BlockSpec(lw["wq_nope"].shape, const),
                  pl.BlockSpec(lw["wq_pe"].shape, const),
                  pl.BlockSpec(lw["wuk_bd"].shape, const),
                  pl.BlockSpec((tm, ROPE_PAD), row),
                  pl.BlockSpec((tm, ROPE_PAD), row)],
        out_specs=[pl.BlockSpec((MLA_HEADS, tm, KV_LORA), lambda i: (0, i, 0)),
                   pl.BlockSpec((MLA_HEADS, tm, ROPE_PAD), lambda i: (0, i, 0)),
                   pl.BlockSpec((tm, KV_LORA), row),
                   pl.BlockSpec((tm, ROPE_PAD), row)],
        out_shape=[jax.ShapeDtypeStruct((MLA_HEADS, m, KV_LORA), F32),
                   jax.ShapeDtypeStruct((MLA_HEADS, m, ROPE_PAD), F32),
                   jax.ShapeDtypeStruct((m, KV_LORA), F32),
                   jax.ShapeDtypeStruct((m, ROPE_PAD), F32)],
        compiler_params=_params("parallel"),
        name="mla_prep",
    )(x, lw["g_norm"], w1, lw["g_q_lat"], lw["g_kv_lat"], lw["wq_nope"], lw["wq_pe"], lw["wuk_bd"], cos_t, sin_t)


def _softmax_step(s, v_bf16, m_sc, l_sc, acc_sc):
    m_prev = m_sc[...]
    m_new = jnp.maximum(m_prev, jnp.max(s, axis=-1, keepdims=True))
    alpha = jnp.exp(m_prev - m_new)
    p = jnp.exp(s - m_new)
    l_sc[...] = alpha * l_sc[...] + jnp.sum(p, axis=-1, keepdims=True)
    acc_sc[...] = alpha * acc_sc[...] + _dot(p.astype(BF16), v_bf16)
    m_sc[...] = m_new


def _mla_prompt_body(qlat_ref, qpe_ref, ckv_ref, kpe_ref, o_ref, m_sc, l_sc, acc_sc, *, tq, tk):
    qi = pl.program_id(1)
    ki = pl.program_id(2)
    last = ((qi + 1) * tq - 1) // tk
    rows = MLA_HEADS * tq

    @pl.when(ki == 0)
    def _():
        m_sc[...] = jnp.full(m_sc.shape, -jnp.inf, F32)
        l_sc[...] = jnp.zeros(l_sc.shape, F32)
        acc_sc[...] = jnp.zeros(acc_sc.shape, F32)

    def step(masked):
        q = qlat_ref[...].reshape(rows, KV_LORA).astype(BF16)
        qp = qpe_ref[...].reshape(rows, ROPE_PAD).astype(BF16)
        k = ckv_ref[...].astype(BF16)
        kp = kpe_ref[...].astype(BF16)
        s = (_dot_nt(q, k) + _dot_nt(qp, kp)) * MLA_SCALE
        if masked:
            q_pos = qi * tq + lax.broadcasted_iota(jnp.int32, (tq, tk), 0)
            k_pos = ki * tk + lax.broadcasted_iota(jnp.int32, (tq, tk), 1)
            s = jnp.where((k_pos <= q_pos)[None], s.reshape(MLA_HEADS, tq, tk), -jnp.inf).reshape(rows, tk)
        _softmax_step(s, k, m_sc, l_sc, acc_sc)

    crosses_diag = (ki + 1) * tk - 1 > qi * tq
    pl.when(jnp.logical_and(ki <= last, crosses_diag))(lambda: step(True))
    pl.when(jnp.logical_and(ki <= last, jnp.logical_not(crosses_diag)))(lambda: step(False))

    @pl.when(ki == last)
    def _():
        o_ref[...] = (acc_sc[...] / l_sc[...]).reshape(MLA_HEADS, tq, KV_LORA)


def mla_prompt(q_lat, q_pe, ckv, kpe, bsz, t):
    tq = min(128, t)
    tk = min(256, t)
    nq, nk = t // tq, t // tk
    rows = MLA_HEADS * tq

    def kv_map(b, qi, ki):
        return (b * nk + jnp.minimum(ki, ((qi + 1) * tq - 1) // tk), 0)

    q_map = lambda b, qi, ki: (0, b * nq + qi, 0)
    return pl.pallas_call(
        functools.partial(_mla_prompt_body, tq=tq, tk=tk),
        grid=(bsz, nq, nk),
        in_specs=[pl.BlockSpec((MLA_HEADS, tq, KV_LORA), q_map),
                  pl.BlockSpec((MLA_HEADS, tq, ROPE_PAD), q_map),
                  pl.BlockSpec((tk, KV_LORA), kv_map),
                  pl.BlockSpec((tk, ROPE_PAD), kv_map)],
        out_specs=pl.BlockSpec((MLA_HEADS, tq, KV_LORA), q_map),
        out_shape=jax.ShapeDtypeStruct((MLA_HEADS, bsz * t, KV_LORA), F32),
        scratch_shapes=[pltpu.VMEM((rows, 1), F32), pltpu.VMEM((rows, 1), F32), pltpu.VMEM((rows, KV_LORA), F32)],
        compiler_params=_params("parallel", "parallel", "arbitrary"),
        name="mla_prompt",
    )(q_lat, q_pe, ckv, kpe)


def _mla_decode_body(pt_ref, qlat_ref, qpe_ref, ckv_new_ref, kpe_new_ref, *rest, t, group):
    kv_refs = rest[:group]
    pe_refs = rest[group:2 * group]
    o_ref, m_sc, l_sc, acc_sc = rest[2 * group:]
    j = pl.program_id(1)
    rows = MLA_HEADS * t
    q = qlat_ref[...].reshape(rows, KV_LORA).astype(BF16)
    qp = qpe_ref[...].reshape(rows, ROPE_PAD).astype(BF16)

    @pl.when(j == 0)
    def _():
        k = jnp.concatenate([ckv_new_ref[...], jnp.zeros((LANES - t, KV_LORA), F32)], axis=0).astype(BF16)
        kp = jnp.concatenate([kpe_new_ref[...], jnp.zeros((LANES - t, ROPE_PAD), F32)], axis=0).astype(BF16)
        s = (_dot_nt(q, k) + _dot_nt(qp, kp)) * MLA_SCALE
        q_t = lax.broadcasted_iota(jnp.int32, (rows, LANES), 0) % t
        k_t = lax.broadcasted_iota(jnp.int32, (rows, LANES), 1)
        s = jnp.where(k_t <= q_t, s, -jnp.inf)
        m = jnp.max(s, axis=-1, keepdims=True)
        p = jnp.exp(s - m)
        m_sc[...] = m
        l_sc[...] = jnp.sum(p, axis=-1, keepdims=True)
        acc_sc[...] = _dot(p.astype(BF16), k)

    @pl.when(j > 0)
    def _():
        qp32 = qp[:, :MLA_ROPE]
        for g in range(group):
            k = kv_refs[g][0, 0].astype(BF16)
            kp = pe_refs[g][0, 0].astype(BF16)
            s = (_dot_nt(q, k) + _dot_nt(qp32, kp)) * MLA_SCALE
            _softmax_step(s, k, m_sc, l_sc, acc_sc)

    @pl.when(j == pl.num_programs(1) - 1)
    def _():
        o_ref[...] = (acc_sc[...] / l_sc[...]).reshape(MLA_HEADS, t, KV_LORA)


def _page_group(n_pages):
    for g in (8, 4, 2, 1):
        if n_pages % g == 0:
            return g


def mla_decode(layer, page_table, q_lat, q_pe, ckv, kpe, cache_kv, cache_pe, bsz, t):
    n_pages = page_table.shape[1]
    group = _page_group(n_pages)
    steps = n_pages // group
    rows = MLA_HEADS * t

    def page_map(g):
        return lambda b, j, pt: (layer, pt[b, jnp.maximum(j - 1, 0) * group + g], 0, 0)

    q_map = lambda b, j, pt: (0, b, 0)
    new_map = lambda b, j, pt: (b, 0)
    grid_spec = pltpu.PrefetchScalarGridSpec(
        num_scalar_prefetch=1,
        grid=(bsz, steps + 1),
        in_specs=[pl.BlockSpec((MLA_HEADS, t, KV_LORA), q_map),
                  pl.BlockSpec((MLA_HEADS, t, ROPE_PAD), q_map),
                  pl.BlockSpec((t, KV_LORA), new_map),
                  pl.BlockSpec((t, ROPE_PAD), new_map)]
                 + [pl.BlockSpec((1, 1, PAGE_SIZE, KV_LORA), page_map(g)) for g in range(group)]
                 + [pl.BlockSpec((1, 1, PAGE_SIZE, MLA_ROPE), page_map(g)) for g in range(group)],
        out_specs=pl.BlockSpec((MLA_HEADS, t, KV_LORA), q_map),
        scratch_shapes=[pltpu.VMEM((rows, 1), F32), pltpu.VMEM((rows, 1), F32), pltpu.VMEM((rows, KV_LORA), F32)],
    )
    return pl.pallas_call(
        functools.partial(_mla_decode_body, t=t, group=group),
        grid_spec=grid_spec,
        out_shape=jax.ShapeDtypeStruct((MLA_HEADS, bsz * t, KV_LORA), F32),
        compiler_params=_params("parallel", "arbitrary"),
        name="mla_decode",
    )(page_table, q_lat, q_pe, ckv, kpe, *([cache_kv] * group), *([cache_pe] * group))


def _tri_ge(n):
    j = lax.broadcasted_iota(jnp.int32, (n, n), 0)
    s = lax.broadcasted_iota(jnp.int32, (n, n), 1)
    return (j >= s).astype(BF16)


def _rev_cumsum(x, tri):
    hi, lo = _split_bf16(x)
    return _dot(hi, tri) + _dot(lo, tri)


def _sb_prompt_body(q_ref, k_ref, v_ref, o_ref, carry_sc, acc_sc, z_sc, lk_sc, *, tq, tk):
    qi = pl.program_id(1)
    kk = pl.program_id(2)
    last = ((qi + 1) * tq - 1) // tk
    ki = last - kk

    @pl.when(kk == 0)
    def _():
        carry_sc[...] = jnp.zeros(carry_sc.shape, F32)
        acc_sc[...] = jnp.zeros(acc_sc.shape, F32)

    def step(masked):
        q = q_ref[...].astype(BF16)
        k = k_ref[...].astype(BF16)
        v = v_ref[...].astype(BF16)
        if masked:
            q_pos = qi * tq + lax.broadcasted_iota(jnp.int32, (tq, tk), 0)
            k_pos = ki * tk + lax.broadcasted_iota(jnp.int32, (tq, tk), 1)
            mask = k_pos < q_pos
        for h in range(SB_HEADS):
            sl = slice(h * SB_HEAD_DIM, (h + 1) * SB_HEAD_DIM)
            z = _dot_nt(q[:, sl], k[:, sl]) * SB_SCALE
            lk = _neg_softplus(z)
            if masked:
                lk = jnp.where(mask, lk, 0.0)
            z_sc[h] = z
            lk_sc[h] = lk
        cum = _rev_cumsum(lk_sc[...].reshape(SB_HEADS * tq, tk), _tri_ge(tk)).reshape(SB_HEADS, tq, tk)
        for h in range(SB_HEADS):
            sl = slice(h * SB_HEAD_DIM, (h + 1) * SB_HEAD_DIM)
            cum_h = cum[h]
            w = jnp.exp(z_sc[h] + carry_sc[h] + cum_h)
            if masked:
                w = jnp.where(mask, w, 0.0)
            acc_sc[h] += _dot(w.astype(BF16), v[:, sl])
            carry_sc[h] += cum_h[:, :1]

    crosses_diag = (ki + 1) * tk > qi * tq
    pl.when(jnp.logical_and(kk <= last, crosses_diag))(lambda: step(True))
    pl.when(jnp.logical_and(kk <= last, jnp.logical_not(crosses_diag)))(lambda: step(False))

    @pl.when(kk == last)
    def _():
        o_ref[...] = jnp.concatenate([acc_sc[h] for h in range(SB_HEADS)], axis=1)


def sb_prompt(p2, bsz, t):
    tq = min(256, t)
    tk = min(256, t)
    nq, nk = t // tq, t // tk

    def kv_map(col):
        def f(b, qi, kk):
            last = ((qi + 1) * tq - 1) // tk
            return (b * nk + jnp.maximum(last - kk, 0), col)
        return f

    return pl.pallas_call(
        functools.partial(_sb_prompt_body, tq=tq, tk=tk),
        grid=(bsz, nq, nk),
        in_specs=[pl.BlockSpec((tq, BRANCH_WIDTH), lambda b, qi, kk: (b * nq + qi, P2_Q_SB)),
                  pl.BlockSpec((tk, BRANCH_WIDTH), kv_map(P2_K_SB)),
                  pl.BlockSpec((tk, BRANCH_WIDTH), kv_map(P2_V_SB))],
        out_specs=pl.BlockSpec((tq, BRANCH_WIDTH), lambda b, qi, kk: (b * nq + qi, 0)),
        out_shape=jax.ShapeDtypeStruct((bsz * t, BRANCH_WIDTH), F32),
        scratch_shapes=[pltpu.VMEM((SB_HEADS, tq, 1), F32),
                        pltpu.VMEM((SB_HEADS, tq, SB_HEAD_DIM), F32),
                        pltpu.VMEM((SB_HEADS, tq, tk), F32),
                        pltpu.VMEM((SB_HEADS, tq, tk), F32)],
        compiler_params=_params("parallel", "parallel", "arbitrary"),
        name="sb_prompt",
    )(p2, p2, p2)


def _sb_decode_body(pt_ref, q_ref, knew_ref, vnew_ref, *rest, t, group):
    k_refs = rest[:group]
    v_refs = rest[group:2 * group]
    o_ref, carry_sc, acc_sc = rest[2 * group:]
    j = pl.program_id(1)
    rows = SB_HEADS * t
    width = BRANCH_WIDTH
    head_of_row = lax.broadcasted_iota(jnp.int32, (rows, width), 0) // t
    head_of_col = lax.broadcasted_iota(jnp.int32, (rows, width), 1) // SB_HEAD_DIM
    own_head = head_of_row == head_of_col
    q = q_ref[...]
    q_bd = jnp.where(own_head, jnp.concatenate([q] * SB_HEADS, axis=0), 0.0).astype(BF16)

    def chunk(k, v, mask):
        n = k.shape[0]
        z = _dot_nt(q_bd, k) * SB_SCALE
        lk = _neg_softplus(z)
        if mask is not None:
            lk = jnp.where(mask, lk, 0.0)
        cum = _rev_cumsum(lk, _tri_ge(n))
        w = jnp.exp(z + carry_sc[...] + cum)
        if mask is not None:
            w = jnp.where(mask, w, 0.0)
        acc_sc[...] += _dot(w.astype(BF16), v)
        carry_sc[...] += cum[:, :1]

    @pl.when(j == 0)
    def _():
        carry_sc[...] = jnp.zeros(carry_sc.shape, F32)
        acc_sc[...] = jnp.zeros(acc_sc.shape, F32)
        pad = jnp.zeros((LANES - t, width), F32)
        k = jnp.concatenate([knew_ref[...], pad], axis=0).astype(BF16)
        v = jnp.concatenate([vnew_ref[...], pad], axis=0).astype(BF16)
        q_t = lax.broadcasted_iota(jnp.int32, (rows, LANES), 0) % t
        k_t = lax.broadcasted_iota(jnp.int32, (rows, LANES), 1)
        chunk(k, v, k_t < q_t)

    @pl.when(j > 0)
    def _():
        for g in reversed(range(group)):
            k = k_refs[g][0, 0].reshape(PAGE_SIZE, width).astype(BF16)
            v = v_refs[g][0, 0].reshape(PAGE_SIZE, width).astype(BF16)
            chunk(k, v, None)

    @pl.when(j == pl.num_programs(1) - 1)
    def _():
        a = jnp.where(own_head, acc_sc[...], 0.0)
        out = a[0:t]
        for h in range(1, SB_HEADS):
            out = out + a[h * t:(h + 1) * t]
        o_ref[...] = out


def sb_decode(layer, page_table, p2, cache_k, cache_v, bsz, t):
    n_pages = page_table.shape[1]
    group = _page_group(n_pages)
    steps = n_pages // group
    rows = SB_HEADS * t

    def page_map(g):
        return lambda b, j, pt: (layer, pt[b, (steps - jnp.maximum(j, 1)) * group + g], 0, 0)

    def col_map(col):
        return lambda b, j, pt: (b, col)

    grid_spec = pltpu.PrefetchScalarGridSpec(
        num_scalar_prefetch=1,
        grid=(bsz, steps + 1),
        in_specs=[pl.BlockSpec((t, BRANCH_WIDTH), col_map(P2_Q_SB)),
                  pl.BlockSpec((t, BRANCH_WIDTH), col_map(P2_K_SB)),
                  pl.BlockSpec((t, BRANCH_WIDTH), col_map(P2_V_SB))]
                 + [pl.BlockSpec((1, 1, PAGE_SIZE, BRANCH_WIDTH), page_map(g)) for g in range(group)]
                 + [pl.BlockSpec((1, 1, PAGE_SIZE, BRANCH_WIDTH), page_map(g)) for g in range(group)],
        out_specs=pl.BlockSpec((t, BRANCH_WIDTH), col_map(0)),
        scratch_shapes=[pltpu.VMEM((rows, 1), F32), pltpu.VMEM((rows, BRANCH_WIDTH), F32)],
    )
    return pl.pallas_call(
        functools.partial(_sb_decode_body, t=t, group=group),
        grid_spec=grid_spec,
        out_shape=jax.ShapeDtypeStruct((bsz * t, BRANCH_WIDTH), F32),
        compiler_params=_params("parallel", "arbitrary"),
        name="sb_decode",
    )(page_table, p2, p2, p2, *([cache_k] * group), *([cache_v] * group))


def _gelu_tanh(x):
    return 0.5 * x * (1.0 + jnp.tanh(math.sqrt(2.0 / math.pi) * (x + 0.044715 * x * x * x)))


def _s5_body(u_ref, h0re_ref, h0im_ref, lbre_ref, lbim_ref, bbre_ref, bbim_ref, cre_ref, cim_ref, d_ref,
             wglu_ref, bglu_ref, o_ref, hre_ref, him_ref, bure_sc, buim_sc, hre_sc, him_sc, *, steps, r, lane_w):
    c = pl.program_id(1)

    @pl.when(c == 0)
    def _():
        hre_sc[...] = h0re_ref[0]
        him_sc[...] = h0im_ref[0]

    u = u_ref[...]
    ub = u.astype(BF16)
    bure_sc[...] = _dot(ub, bbre_ref[...])
    buim_sc[...] = _dot(ub, bbim_ref[...])

    for lc in range(S5_WIDTH // lane_w):
        ls = pl.ds(lc * lane_w, lane_w)
        lb_re = jnp.broadcast_to(lbre_ref[:, ls], (r, lane_w))
        lb_im = jnp.broadcast_to(lbim_ref[:, ls], (r, lane_w))

        def body(i, carry):
            h_re, h_im = carry
            rs = pl.ds(pl.multiple_of(i * r, r), r)
            n_re = lb_re * h_re - lb_im * h_im + bure_sc[rs, ls]
            n_im = lb_re * h_im + lb_im * h_re + buim_sc[rs, ls]
            bure_sc[rs, ls] = n_re
            buim_sc[rs, ls] = n_im
            return n_re, n_im

        h_re, h_im = lax.fori_loop(0, steps, body, (hre_sc[:, ls], him_sc[:, ls]))
        hre_sc[:, ls] = h_re
        him_sc[:, ls] = h_im

    y = (_dot(bure_sc[...].astype(BF16), cre_ref[...]) - _dot(buim_sc[...].astype(BF16), cim_ref[...])
         + d_ref[...] * u)
    zz = _dot(_gelu_tanh(y).astype(BF16), wglu_ref[...]) + bglu_ref[...]
    o_ref[...] = zz[:, :BRANCH_WIDTH] * jax.nn.sigmoid(zz[:, BRANCH_WIDTH:])

    @pl.when(c == pl.num_programs(1) - 1)
    def _():
        hre_ref[0] = hre_sc[...]
        him_ref[0] = him_sc[...]


def s5_branch(u_src, u_col, h0_re, h0_im, lw, nblk, nchunk, steps, r, lane_w):
    rows = steps * r
    const = lambda b, c: (0, 0)
    blk = lambda b, c: (b, 0, 0)
    return pl.pallas_call(
        functools.partial(_s5_body, steps=steps, r=r, lane_w=lane_w),
        grid=(nblk, nchunk),
        in_specs=[pl.BlockSpec((rows, BRANCH_WIDTH), lambda b, c: (b * nchunk + c, u_col)),
                  pl.BlockSpec((1, r, S5_WIDTH), blk),
                  pl.BlockSpec((1, r, S5_WIDTH), blk),
                  pl.BlockSpec((1, S5_WIDTH), const),
                  pl.BlockSpec((1, S5_WIDTH), const),
                  pl.BlockSpec((BRANCH_WIDTH, S5_WIDTH), const),
                  pl.BlockSpec((BRANCH_WIDTH, S5_WIDTH), const),
                  pl.BlockSpec((S5_WIDTH, BRANCH_WIDTH), const),
                  pl.BlockSpec((S5_WIDTH, BRANCH_WIDTH), const),
                  pl.BlockSpec((1, BRANCH_WIDTH), const),
                  pl.BlockSpec((BRANCH_WIDTH, 2 * BRANCH_WIDTH), const),
                  pl.BlockSpec((1, 2 * BRANCH_WIDTH), const)],
        out_specs=[pl.BlockSpec((rows, BRANCH_WIDTH), lambda b, c: (b * nchunk + c, 0)),
                   pl.BlockSpec((1, r, S5_WIDTH), blk),
                   pl.BlockSpec((1, r, S5_WIDTH), blk)],
        out_shape=[jax.ShapeDtypeStruct((nblk * nchunk * rows, BRANCH_WIDTH), F32),
                   jax.ShapeDtypeStruct((nblk, r, S5_WIDTH), F32),
                   jax.ShapeDtypeStruct((nblk, r, S5_WIDTH), F32)],
        scratch_shapes=[pltpu.VMEM((rows, S5_WIDTH), F32), pltpu.VMEM((rows, S5_WIDTH), F32),
                        pltpu.VMEM((r, S5_WIDTH), F32), pltpu.VMEM((r, S5_WIDTH), F32)],
        compiler_params=_params("parallel", "arbitrary"),
        name="s5_branch",
    )(u_src, h0_re, h0_im, lw["lb_re"], lw["lb_im"], lw["bb_re"], lw["bb_im"], lw["c_re"], lw["c_im"],
      lw["s5_d"], lw["w_glu"], lw["b_glu"])


def _xattn_body(q_ref, k_ref, v_ref, o_ref, *, nb):
    for i in range(nb):
        q = q_ref[i].astype(BF16)
        k = k_ref[i].astype(BF16)
        v = v_ref[i].astype(BF16)
        outs = []
        for h in range(X_HEADS):
            sl = slice(h * X_HEAD_DIM, (h + 1) * X_HEAD_DIM)
            s = _dot_nt(q[:, sl], k[:, sl]) * X_SCALE
            p = jnp.exp(s - jnp.max(s, axis=-1, keepdims=True))
            p = p / jnp.sum(p, axis=-1, keepdims=True)
            outs.append(_dot(p.astype(BF16), v[:, sl]))
        o_ref[i] = jnp.concatenate(outs, axis=1)


def cross_attend(p2, mem_k, mem_v, bsz, t):
    n_mem = mem_k.shape[1]
    tq = min(512, t)
    nb = 8 if (t <= 8 and bsz % 8 == 0) else 1
    p3 = p2.reshape(bsz, t, P2_WIDTH)
    mem_map = lambda b, qi: (b, 0, 0)
    out = pl.pallas_call(
        functools.partial(_xattn_body, nb=nb),
        grid=(bsz // nb, t // tq),
        in_specs=[pl.BlockSpec((nb, tq, BRANCH_WIDTH), lambda b, qi: (b, qi, P2_Q_X)),
                  pl.BlockSpec((nb, n_mem, BRANCH_WIDTH), mem_map),
                  pl.BlockSpec((nb, n_mem, BRANCH_WIDTH), mem_map)],
        out_specs=pl.BlockSpec((nb, tq, BRANCH_WIDTH), lambda b, qi: (b, qi, 0)),
        out_shape=jax.ShapeDtypeStruct((bsz, t, BRANCH_WIDTH), F32),
        compiler_params=_params("parallel", "parallel"),
        name="cross_attend",
    )(p3, mem_k, mem_v)
    return out.reshape(bsz * t, BRANCH_WIDTH)


def _merge_body(x_ref, olat_ref, os5_ref, osb_ref, ox_ref, gmla_ref, gs5_ref, gsb_ref, gx_ref, gm_ref,
                wuv_ref, wbr_ref, wout_ref, gfin_ref, o_ref, *, final):
    o_mla = _dot(olat_ref[0].astype(BF16), wuv_ref[0])
    for h in range(1, MLA_HEADS):
        o_mla = o_mla + _dot(olat_ref[h].astype(BF16), wuv_ref[h])
    branches = ((o_mla, gmla_ref), (os5_ref[...], gs5_ref), (osb_ref[...], gsb_ref), (ox_ref[...], gx_ref))
    merged = None
    for n, (o, g_ref) in enumerate(branches):
        br = (o * jax.nn.silu(g_ref[...])).astype(BF16)
        proj = _dot(br, wbr_ref[n])
        term = jax.nn.sigmoid(gm_ref[:, n * D_MODEL:(n + 1) * D_MODEL]) * proj
        merged = term if merged is None else merged + term
    x = x_ref[...] + _dot(merged.astype(BF16), wout_ref[...])
    o_ref[...] = _rms(x, gfin_ref[...]) if final else x


def merge_out(x, o_lat, o_s5, o_sb, o_x, p2, lw, g_final, final):
    m, d = x.shape
    tm = min(256, m)
    row = lambda i: (i, 0)
    const2 = lambda i: (0, 0)
    const3 = lambda i: (0, 0, 0)
    col = lambda c: (lambda i: (i, c))
    bw = pl.BlockSpec((tm, BRANCH_WIDTH), row)
    return pl.pallas_call(
        functools.partial(_merge_body, final=final),
        grid=(m // tm,),
        in_specs=[pl.BlockSpec((tm, d), row),
                  pl.BlockSpec((MLA_HEADS, tm, KV_LORA), lambda i: (0, i, 0)),
                  bw, bw, bw,
                  pl.BlockSpec((tm, BRANCH_WIDTH), col(P2_G_MLA)),
                  pl.BlockSpec((tm, BRANCH_WIDTH), col(P2_G_S5)),
                  pl.BlockSpec((tm, BRANCH_WIDTH), col(P2_G_SB)),
                  pl.BlockSpec((tm, BRANCH_WIDTH), col(P2_G_X)),
                  pl.BlockSpec((tm, P2_GMERGE_W), col(0)),
                  pl.BlockSpec(lw["wuv_pad"].shape, const3),
                  pl.BlockSpec(lw["w_branch"].shape, const3),
                  pl.BlockSpec(lw["w_out"].shape, const2),
                  pl.BlockSpec((1, d), const2)],
        out_specs=pl.BlockSpec((tm, d), row),
        out_shape=jax.ShapeDtypeStruct((m, d), F32),
        compiler_params=_params("parallel"),
        name="merge_out",
    )(x, o_lat, o_s5, o_sb, o_x, p2, p2, p2, p2, p2, lw["wuv_pad"], lw["w_branch"], lw["w_out"], g_final)


def _block_diag(blocks):
    g, r, c = blocks.shape
    eye = jnp.eye(g, dtype=blocks.dtype)
    return (eye[:, None, :, None] * blocks[:, :, None, :]).reshape(g * r, g * c)


def _layer_weights(l, g_norm, w_in, g_q_lat, g_kv_lat, w_uq, w_uk, w_uv, s5_a_re, s5_a_im, s5_log_dt,
                   s5_b_re, s5_b_im, s5_c_re, s5_c_im, s5_d, w_glu, b_glu, w_branch, w_out):
    w = w_in[l]
    o1 = Q_LORA + KV_LORA
    o2 = o1 + MLA_ROPE
    o3 = o2 + 9 * BRANCH_WIDTH
    rope_pad = jnp.zeros((D_MODEL, ROPE_PAD - MLA_ROPE), F32)
    w1 = jnp.concatenate([w[:, :o2], rope_pad], axis=1).astype(BF16)
    w2 = jnp.concatenate([w[:, o3:], w[:, o2:o3]], axis=1).astype(BF16)
    hd = MLA_NOPE + MLA_ROPE
    wq = w_uq[l].reshape(Q_LORA, MLA_HEADS, hd)
    wq_nope = wq[:, :, :MLA_NOPE].reshape(Q_LORA, MLA_HEADS * MLA_NOPE).astype(BF16)
    wq_pe = jnp.pad(wq[:, :, MLA_NOPE:], ((0, 0), (0, 0), (0, ROPE_PAD - MLA_ROPE))).reshape(
        Q_LORA, MLA_HEADS * ROPE_PAD).astype(BF16)
    wuk_bd = _block_diag(jnp.transpose(w_uk[l], (1, 2, 0))).astype(BF16)
    wuv = jnp.transpose(w_uv[l], (1, 0, 2))
    eye_h = jnp.eye(MLA_HEADS, dtype=F32)
    wuv_pad = (wuv[:, :, None, :] * eye_h[:, None, :, None]).reshape(MLA_HEADS, KV_LORA, BRANCH_WIDTH).astype(BF16)
    a_re, a_im = s5_a_re[l].astype(F32), s5_a_im[l].astype(F32)
    dt = jnp.exp(s5_log_dt[l].astype(F32))[:, None]
    mag = jnp.exp(a_re * dt)
    lb_re, lb_im = mag * jnp.cos(a_im * dt), mag * jnp.sin(a_im * dt)
    den = a_re * a_re + a_im * a_im
    n_re = lb_re - 1.0
    f_re = (n_re * a_re + lb_im * a_im) / den
    f_im = (lb_im * a_re - n_re * a_im) / den
    b_re, b_im = s5_b_re[l].astype(F32), s5_b_im[l].astype(F32)
    bb_re = f_re[..., None] * b_re - f_im[..., None] * b_im
    bb_im = f_re[..., None] * b_im + f_im[..., None] * b_re
    return dict(
        g_norm=g_norm[l].reshape(1, D_MODEL), w1=w1, w2=w2,
        g_q_lat=g_q_lat[l].reshape(1, Q_LORA), g_kv_lat=g_kv_lat[l].reshape(1, KV_LORA),
        wq_nope=wq_nope, wq_pe=wq_pe, wuk_bd=wuk_bd, wuv_pad=wuv_pad,
        lb_re=lb_re.reshape(1, S5_WIDTH), lb_im=lb_im.reshape(1, S5_WIDTH),
        bb_re=_block_diag(jnp.transpose(bb_re, (0, 2, 1))).astype(BF16),
        bb_im=_block_diag(jnp.transpose(bb_im, (0, 2, 1))).astype(BF16),
        c_re=_block_diag(jnp.transpose(s5_c_re[l].astype(F32), (0, 2, 1))).astype(BF16),
        c_im=_block_diag(jnp.transpose(s5_c_im[l].astype(F32), (0, 2, 1))).astype(BF16),
        s5_d=s5_d[l].reshape(1, BRANCH_WIDTH).astype(F32),
        w_glu=w_glu[l].astype(BF16), b_glu=b_glu[l].reshape(1, 2 * BRANCH_WIDTH),
        w_branch=w_branch[l].astype(BF16), w_out=w_out[l].astype(BF16))


def _rope_tables(pos):
    half = MLA_ROPE // 2
    freq = ROPE_BASE ** (-jnp.arange(half, dtype=F32) / half)
    ang = pos.astype(F32)[:, None] * freq
    cos, sin = jnp.cos(ang), jnp.sin(ang)
    pad = jnp.zeros((pos.shape[0], ROPE_PAD - MLA_ROPE), F32)
    return jnp.concatenate([cos, cos, pad], axis=1), jnp.concatenate([-sin, sin, pad], axis=1)


def _branches(x, lw, cos_t, sin_t):
    p2 = norm_matmul(x, lw["g_norm"], lw["w2"], tn=P2_WIDTH // 4)
    q_lat, q_pe, ckv, kpe = mla_prep(x, lw, cos_t, sin_t)
    return p2, q_lat, q_pe, ckv, kpe


def kernel(x_prompt, x_sample, mem_prompt, cache_mla_kv, cache_mla_pe, cache_sb_k, cache_sb_v, state_s5_re, state_s5_im, cache_mem_k, cache_mem_v, page_table, g_norm, w_in, g_q_lat, g_kv_lat, w_uq, w_uk, w_uv, s5_a_re, s5_a_im, s5_log_dt, s5_b_re, s5_b_im, s5_c_re, s5_c_im, s5_d, w_glu, b_glu, g_mem, w_mem_kv, w_branch, w_out, g_final):
    depth = w_in.shape[0]
    bp, tp, d = x_prompt.shape
    bs, ts, _ = x_sample.shape
    n_mem = mem_prompt.shape[1]
    n_pool = cache_mla_kv.shape[1]
    past_len = page_table.shape[1] * PAGE_SIZE
    mp, ms = bp * tp, bs * ts

    cos_p, sin_p = _rope_tables(jnp.arange(tp, dtype=jnp.int32))
    cos_p, sin_p = jnp.tile(cos_p, (bp, 1)), jnp.tile(sin_p, (bp, 1))
    cos_s, sin_s = _rope_tables(past_len + jnp.arange(ts, dtype=jnp.int32))
    cos_s, sin_s = jnp.tile(cos_s, (bs, 1)), jnp.tile(sin_s, (bs, 1))

    cache_k4 = cache_sb_k.reshape(depth, n_pool, PAGE_SIZE, BRANCH_WIDTH)
    cache_v4 = cache_sb_v.reshape(depth, n_pool, PAGE_SIZE, BRANCH_WIDTH)
    page_table = page_table.astype(jnp.int32)
    g_fin = g_final.reshape(1, d)
    zeros_p = jnp.zeros((bp, 1, S5_WIDTH), F32)

    r_s = 32 if bs % 32 == 0 else bs
    nblk_s = bs // r_s
    chunk_p = min(256, tp)

    xp = x_prompt.reshape(mp, d)
    xs = x_sample.reshape(ms, d)
    mem = mem_prompt.reshape(bp * n_mem, d)
    new_p = [[] for _ in range(8)]
    new_s = [[] for _ in range(6)]
    for l in range(depth):
        lw = _layer_weights(l, g_norm, w_in, g_q_lat, g_kv_lat, w_uq, w_uk, w_uv, s5_a_re, s5_a_im, s5_log_dt,
                            s5_b_re, s5_b_im, s5_c_re, s5_c_im, s5_d, w_glu, b_glu, w_branch, w_out)
        final = l == depth - 1
        mkv = norm_matmul(mem, g_mem[l], w_mem_kv[l].astype(BF16), tn=2 * BRANCH_WIDTH)
        mk = mkv[:, :BRANCH_WIDTH].reshape(bp, n_mem, BRANCH_WIDTH)
        mv = mkv[:, BRANCH_WIDTH:].reshape(bp, n_mem, BRANCH_WIDTH)
        p2, q_lat, q_pe, ckv, kpe = _branches(xp, lw, cos_p, sin_p)
        o_lat = mla_prompt(q_lat, q_pe, ckv, kpe, bp, tp)
        o_sb = sb_prompt(p2, bp, tp)
        o_s5, h_re, h_im = s5_branch(p2, P2_U_S5, zeros_p, zeros_p, lw, bp, tp // chunk_p, chunk_p, 1,
                                     S5_WIDTH // 2)
        o_x = cross_attend(p2, mk, mv, bp, tp)
        xp = merge_out(xp, o_lat, o_s5, o_sb, o_x, p2, lw, g_fin, final)
        st_p = (ckv.reshape(bp, tp, KV_LORA), kpe[:, :MLA_ROPE].reshape(bp, tp, MLA_ROPE),
                p2[:, P2_K_SB * BRANCH_WIDTH:(P2_K_SB + 1) * BRANCH_WIDTH].reshape(bp, tp, SB_HEADS, SB_HEAD_DIM),
                p2[:, P2_V_SB * BRANCH_WIDTH:(P2_V_SB + 1) * BRANCH_WIDTH].reshape(bp, tp, SB_HEADS, SB_HEAD_DIM),
                h_re.reshape(bp, S5_GROUPS, S5_STATE), h_im.reshape(bp, S5_GROUPS, S5_STATE),
                mk.reshape(bp, n_mem, X_HEADS, X_HEAD_DIM), mv.reshape(bp, n_mem, X_HEADS, X_HEAD_DIM))
        p2, q_lat, q_pe, ckv, kpe = _branches(xs, lw, cos_s, sin_s)
        o_lat = mla_decode(l, page_table, q_lat, q_pe, ckv, kpe, cache_mla_kv, cache_mla_pe, bs, ts)
        o_sb = sb_decode(l, page_table, p2, cache_k4, cache_v4, bs, ts)
        u_s = p2[:, P2_U_S5 * BRANCH_WIDTH:(P2_U_S5 + 1) * BRANCH_WIDTH]
        u_s = u_s.reshape(nblk_s, r_s, ts, BRANCH_WIDTH).transpose(0, 2, 1, 3).reshape(ms, BRANCH_WIDTH)
        o_s5, h_re, h_im = s5_branch(u_s, 0, state_s5_re[l].reshape(nblk_s, r_s, S5_WIDTH),
                                     state_s5_im[l].reshape(nblk_s, r_s, S5_WIDTH), lw, nblk_s, 1, ts, r_s,
                                     min(512, S5_WIDTH))
        o_s5 = o_s5.reshape(nblk_s, ts, r_s, BRANCH_WIDTH).transpose(0, 2, 1, 3).reshape(ms, BRANCH_WIDTH)
        o_x = cross_attend(p2, cache_mem_k[l].reshape(bs, n_mem, BRANCH_WIDTH),
                           cache_mem_v[l].reshape(bs, n_mem, BRANCH_WIDTH), bs, ts)
        xs = merge_out(xs, o_lat, o_s5, o_sb, o_x, p2, lw, g_fin, final)
        st_s = (ckv.reshape(bs, ts, KV_LORA), kpe[:, :MLA_ROPE].reshape(bs, ts, MLA_ROPE),
                p2[:, P2_K_SB * BRANCH_WIDTH:(P2_K_SB + 1) * BRANCH_WIDTH].reshape(bs, ts, SB_HEADS, SB_HEAD_DIM),
                p2[:, P2_V_SB * BRANCH_WIDTH:(P2_V_SB + 1) * BRANCH_WIDTH].reshape(bs, ts, SB_HEADS, SB_HEAD_DIM),
                h_re.reshape(bs, S5_GROUPS, S5_STATE), h_im.reshape(bs, S5_GROUPS, S5_STATE))
        for lst, a in zip(new_p, st_p):
            lst.append(a)
        for lst, a in zip(new_s, st_s):
            lst.append(a)
    outs_p = [jnp.stack(a) for a in new_p]
    outs_s = [jnp.stack(a) for a in new_s]
    return (xp.reshape(bp, tp, d), xs.reshape(bs, ts, d), *outs_p, *outs_s)
```

```python
import functools
import math

import jax
import jax.numpy as jnp
from jax import lax
from jax.experimental import pallas as pl
from jax.experimental.pallas import tpu as pltpu

F32 = jnp.float32
BF16 = jnp.bfloat16

D_MODEL = 1024
BRANCH_WIDTH = D_MODEL // 2
N_BRANCH = 4
MLA_HEADS = 8
MLA_NOPE = 64
MLA_ROPE = 32
MLA_V_DIM = BRANCH_WIDTH // MLA_HEADS
Q_LORA = 3 * D_MODEL // 8
KV_LORA = D_MODEL // 4
MLA_SCALE = (MLA_NOPE + MLA_ROPE) ** -0.5
ROPE_BASE = 10000.0
S5_GROUP = 16
S5_GROUPS = BRANCH_WIDTH // S5_GROUP
S5_STATE = 64
S5_WIDTH = S5_GROUPS * S5_STATE
SB_HEADS = 8
SB_HEAD_DIM = BRANCH_WIDTH // SB_HEADS
SB_SCALE = SB_HEAD_DIM ** -0.5
X_HEADS = 4
X_HEAD_DIM = BRANCH_WIDTH // X_HEADS
X_SCALE = X_HEAD_DIM ** -0.5
RMS_EPS = 1e-6
PAGE_SIZE = 128

LANES = 128
ROPE_PAD = LANES
QK_WIDTH = KV_LORA + ROPE_PAD
LOG2E = math.log2(math.e)
MLA_DECODE_SPLIT = 1
VMEM_LIMIT = 56 * 1024 * 1024

P2_GMERGE_W = N_BRANCH * D_MODEL
_P2_BASE = P2_GMERGE_W // BRANCH_WIDTH
P2_G_MLA, P2_U_S5, P2_G_S5, P2_Q_SB, P2_K_SB, P2_V_SB, P2_G_SB, P2_Q_X, P2_G_X = (
    _P2_BASE + i for i in range(9))
P2_WIDTH = P2_GMERGE_W + 9 * BRANCH_WIDTH


def _params(*sem):
    return pltpu.CompilerParams(dimension_semantics=sem, vmem_limit_bytes=VMEM_LIMIT)


def _rms(x, g):
    return x * lax.rsqrt(jnp.mean(x * x, axis=-1, keepdims=True) + RMS_EPS) * g


def _dot(a, b):
    return jnp.dot(a, b, preferred_element_type=F32)


def _dot_nt(a, b):
    return lax.dot_general(a, b, (((1,), (1,)), ((), ())), preferred_element_type=F32)


def _split_bf16(x):
    hi = x.astype(BF16)
    lo = (x - hi.astype(F32)).astype(BF16)
    return hi, lo


def _neg_softplus(z):
    return -(jnp.maximum(z, 0.0) + jnp.log(1.0 + jnp.exp(-jnp.abs(z))))


def _norm_mm_body(x_ref, g_ref, w_ref, o_ref):
    h = _rms(x_ref[...], g_ref[...]).astype(BF16)
    o_ref[...] = _dot(h, w_ref[...])


def norm_matmul(x, g, w, tn):
    m, d = x.shape
    n = w.shape[1]
    tm = min(512, m)
    return pl.pallas_call(
        _norm_mm_body,
        grid=(n // tn, m // tm),
        in_specs=[pl.BlockSpec((tm, d), lambda j, i: (i, 0)),
                  pl.BlockSpec((1, d), lambda j, i: (0, 0)),
                  pl.BlockSpec((d, tn), lambda j, i: (0, j))],
        out_specs=pl.BlockSpec((tm, tn), lambda j, i: (i, j)),
        out_shape=jax.ShapeDtypeStruct((m, n), F32),
        compiler_params=_params("parallel", "parallel"),
        name="norm_matmul",
    )(x, g.reshape(1, d), w)


def _rope_padded(x, cos_t, sin_t):
    n = x.shape[-1]
    half = MLA_ROPE // 2
    lane = lax.broadcasted_iota(jnp.int32, x.shape, 1) % ROPE_PAD
    swapped = jnp.where(lane < half, pltpu.roll(x, n - half, 1), pltpu.roll(x, half, 1))
    reps = n // ROPE_PAD
    if reps > 1:
        cos_t = jnp.concatenate([cos_t] * reps, axis=1)
        sin_t = jnp.concatenate([sin_t] * reps, axis=1)
    return x * cos_t + swapped * sin_t


def _mla_prep_body(x_ref, gn_ref, w1_ref, gq_ref, gkv_ref, wqn_ref, wqp_ref, wuk_ref, cos_ref, sin_ref,
                   qcat_ref, ckv_ref, kpe_ref, kcat_ref):
    h = _rms(x_ref[...], gn_ref[...]).astype(BF16)
    c = _dot(h, w1_ref[...])
    cq = c[:, :Q_LORA]
    ckv_raw = c[:, Q_LORA:Q_LORA + KV_LORA]
    kpe_raw = c[:, Q_LORA + KV_LORA:]
    cos_t = cos_ref[...]
    sin_t = sin_ref[...]
    ckv = _rms(ckv_raw, gkv_ref[...])
    kpe = _rope_padded(kpe_raw, cos_t, sin_t)
    ckv_ref[...] = ckv
    kpe_ref[...] = kpe
    kcat_ref[...] = jnp.concatenate([ckv, kpe], axis=1).astype(BF16)
    qn = _rms(cq, gq_ref[...]).astype(BF16)
    q_nope = _dot(qn, wqn_ref[...])
    q_pe = _rope_padded(_dot(qn, wqp_ref[...]), cos_t, sin_t)
    q_lat = _dot(q_nope.astype(BF16), wuk_ref[...])
    for hh in range(MLA_HEADS):
        qcat_ref[hh] = jnp.concatenate([q_lat[:, hh * KV_LORA:(hh + 1) * KV_LORA],
                                        q_pe[:, hh * ROPE_PAD:(hh + 1) * ROPE_PAD]], axis=1).astype(qcat_ref.dtype)


def mla_prep(x, lw, cos_t, sin_t, q_dtype):
    m, d = x.shape
    tm = min(256, m)
    row = lambda i: (i, 0)
    const = lambda i: (0, 0)
    w1 = lw["w1"]
    return pl.pallas_call(
        _mla_prep_body,
        grid=(m // tm,),
        in_specs=[pl.BlockSpec((tm, d), row),
                  pl.BlockSpec((1, d), const),
                  pl.BlockSpec(w1.shape, const),
                  pl.BlockSpec((1, Q_LORA), const),
                  pl.BlockSpec((1, KV_LORA), const),
                  pl.BlockSpec(lw["wq_nope"].shape, const),
                  pl.BlockSpec(lw["wq_pe"].shape, const),
                  pl.BlockSpec(lw["wuk_bd"].shape, const),
                  pl.BlockSpec((tm, ROPE_PAD), row),
                  pl.BlockSpec((tm, ROPE_PAD), row)],
        out_specs=[pl.BlockSpec((MLA_HEADS, tm, QK_WIDTH), lambda i: (0, i, 0)),
                   pl.BlockSpec((tm, KV_LORA), row),
                   pl.BlockSpec((tm, ROPE_PAD), row),
                   pl.BlockSpec((tm, QK_WIDTH), row)],
        out_shape=[jax.ShapeDtypeStruct((MLA_HEADS, m, QK_WIDTH), q_dtype),
                   jax.ShapeDtypeStruct((m, KV_LORA), F32),
                   jax.ShapeDtypeStruct((m, ROPE_PAD), F32),
                   jax.ShapeDtypeStruct((m, QK_WIDTH), BF16)],
        compiler_params=_params("parallel"),
        name="mla_prep",
    )(x, lw["g_norm"], w1, lw["g_q_lat"], lw["g_kv_lat"], lw["wq_nope"], lw["wq_pe"], lw["wuk_bd"], cos_t, sin_t)


def _softmax_step(s, v_bf16, m_sc, l_sc, acc_sc):
    m_prev = m_sc[...]
    m_new = jnp.maximum(m_prev, jnp.max(s, axis=-1, keepdims=True))
    alpha = jnp.exp(m_prev - m_new)
    p = jnp.exp(s - m_new)
    l_sc[...] = alpha * l_sc[...] + jnp.sum(p, axis=-1, keepdims=True)
    acc_sc[...] = alpha * acc_sc[...] + _dot(p.astype(BF16), v_bf16)
    m_sc[...] = m_new


def _mla_prompt_body(q_ref, k_ref, o_ref, m_sc, l_sc, acc_sc, *, tq, tk):
    qi = pl.program_id(1)
    ki = pl.program_id(2)
    last = ((qi + 1) * tq - 1) // tk
    rows = MLA_HEADS * tq
    c = MLA_SCALE * LOG2E

    @pl.when(ki == 0)
    def _():
        m_sc[...] = jnp.full(m_sc.shape, -jnp.inf, F32)
        l_sc[...] = jnp.zeros(l_sc.shape, F32)
        acc_sc[...] = jnp.zeros(acc_sc.shape, F32)

    def step(masked):
        q = q_ref[...].reshape(rows, QK_WIDTH)
        k = k_ref[...]
        s = _dot_nt(q, k)
        if masked:
            q_pos = qi * tq + lax.broadcasted_iota(jnp.int32, (tq, tk), 0)
            k_pos = ki * tk + lax.broadcasted_iota(jnp.int32, (tq, tk), 1)
            s = jnp.where((k_pos <= q_pos)[None], s.reshape(MLA_HEADS, tq, tk), -jnp.inf).reshape(rows, tk)
        m_prev = m_sc[...]
        m_new = jnp.maximum(m_prev, jnp.max(s, axis=-1, keepdims=True))
        alpha = jnp.exp2((m_prev - m_new) * c)
        p = jnp.exp2((s - jnp.concatenate([m_new] * (tk // LANES), axis=1)) * c)
        p_lanes = p[:, :LANES]
        for i in range(1, tk // LANES):
            p_lanes = p_lanes + p[:, i * LANES:(i + 1) * LANES]
        l_sc[...] = alpha * l_sc[...] + p_lanes
        acc_sc[...] = (jnp.concatenate([alpha] * (KV_LORA // LANES), axis=1) * acc_sc[...]
                       + _dot(p.astype(BF16), k[:, :KV_LORA]))
        m_sc[...] = m_new

    crosses_diag = (ki + 1) * tk - 1 > qi * tq
    pl.when(jnp.logical_and(ki <= last, crosses_diag))(lambda: step(True))
    pl.when(jnp.logical_and(ki <= last, jnp.logical_not(crosses_diag)))(lambda: step(False))

    @pl.when(ki == last)
    def _():
        l = jnp.sum(l_sc[...], axis=-1, keepdims=True)
        o_ref[...] = (acc_sc[...] / l).reshape(MLA_HEADS, tq, KV_LORA).astype(o_ref.dtype)


def mla_prompt(q_cat, k_cat, bsz, t):
    tq = min(128, t)
    tk = min(512, t)
    nq, nk = t // tq, t // tk
    rows = MLA_HEADS * tq

    def kv_map(b, qi, ki):
        return (b * nk + jnp.minimum(ki, ((qi + 1) * tq - 1) // tk), 0)

    q_map = lambda b, qi, ki: (0, b * nq + qi, 0)
    return pl.pallas_call(
        functools.partial(_mla_prompt_body, tq=tq, tk=tk),
        grid=(bsz, nq, nk),
        in_specs=[pl.BlockSpec((MLA_HEADS, tq, QK_WIDTH), q_map),
                  pl.BlockSpec((tk, QK_WIDTH), kv_map)],
        out_specs=pl.BlockSpec((MLA_HEADS, tq, KV_LORA), q_map),
        out_shape=jax.ShapeDtypeStruct((MLA_HEADS, bsz * t, KV_LORA), BF16),
        scratch_shapes=[pltpu.VMEM((rows, LANES), F32), pltpu.VMEM((rows, LANES), F32),
                        pltpu.VMEM((rows, KV_LORA), F32)],
        compiler_params=_params("parallel", "parallel", "arbitrary"),
        name="mla_prompt",
    )(q_cat, k_cat)


def _mla_decode_body(pt_ref, q_ref, ckv_new_ref, kpe_new_ref, *rest, t, group):
    kv_refs = rest[:group]
    pe_refs = rest[group:2 * group]
    o_ref, m_sc, l_sc, acc_sc = rest[2 * group:]
    j = pl.program_id(1)
    rows = MLA_HEADS * t
    q = q_ref[...].reshape(rows, QK_WIDTH).astype(BF16)
    ql = q[:, :KV_LORA]
    qp = q[:, KV_LORA:]

    @pl.when(j == 0)
    def _():
        k = jnp.concatenate([ckv_new_ref[...], jnp.zeros((LANES - t, KV_LORA), F32)], axis=0).astype(BF16)
        kp = jnp.concatenate([kpe_new_ref[...], jnp.zeros((LANES - t, ROPE_PAD), F32)], axis=0).astype(BF16)
        s = (_dot_nt(ql, k) + _dot_nt(qp, kp)) * MLA_SCALE
        q_t = lax.broadcasted_iota(jnp.int32, (rows, LANES), 0) % t
        k_t = lax.broadcasted_iota(jnp.int32, (rows, LANES), 1)
        s = jnp.where(k_t <= q_t, s, -jnp.inf)
        m = jnp.max(s, axis=-1, keepdims=True)
        p = jnp.exp(s - m)
        m_sc[...] = m
        l_sc[...] = jnp.sum(p, axis=-1, keepdims=True)
        acc_sc[...] = _dot(p.astype(BF16), k)

    parts = []
    per = max(group // MLA_DECODE_SPLIT, 1)
    for g0 in range(0, group, per):
        k = jnp.concatenate([kv_refs[g][0, 0].astype(BF16) for g in range(g0, g0 + per)], axis=0)
        kpe_t = jnp.concatenate([pe_refs[g][0, 0].astype(BF16) for g in range(g0, g0 + per)], axis=1)
        s = (_dot_nt(ql, k) + _dot(qp[:, :MLA_ROPE], kpe_t)) * MLA_SCALE
        m = jnp.max(s, axis=-1, keepdims=True)
        p = jnp.exp(s - m)
        parts.append((m, jnp.sum(p, axis=-1, keepdims=True), _dot(p.astype(BF16), k)))
    m_prev = m_sc[...]
    m_new = m_prev
    for m, _, _ in parts:
        m_new = jnp.maximum(m_new, m)
    alpha = jnp.exp(m_prev - m_new)
    l_new = alpha * l_sc[...]
    acc_new = alpha * acc_sc[...]
    for m, l, acc in parts:
        beta = jnp.exp(m - m_new)
        l_new = l_new + beta * l
        acc_new = acc_new + beta * acc
    m_sc[...] = m_new
    l_sc[...] = l_new
    acc_sc[...] = acc_new

    @pl.when(j == pl.num_programs(1) - 1)
    def _():
        o_ref[...] = (acc_sc[...] / l_sc[...]).reshape(MLA_HEADS, t, KV_LORA)


def _page_group(n_pages, largest):
    g = largest
    while n_pages % g:
        g //= 2
    return g


def mla_decode(layer, page_table, q_cat, ckv, kpe, cache_kv, cache_pe_t, bsz, t):
    n_pages = page_table.shape[1]
    group = _page_group(n_pages, 16)
    steps = n_pages // group
    rows = MLA_HEADS * t

    def page_map(g):
        return lambda b, j, pt: (layer, pt[b, j * group + g], 0, 0)

    q_map = lambda b, j, pt: (0, b, 0)
    new_map = lambda b, j, pt: (b, 0)
    grid_spec = pltpu.PrefetchScalarGridSpec(
        num_scalar_prefetch=1,
        grid=(bsz, steps),
        in_specs=[pl.BlockSpec((MLA_HEADS, t, QK_WIDTH), q_map),
                  pl.BlockSpec((t, KV_LORA), new_map),
                  pl.BlockSpec((t, ROPE_PAD), new_map)]
                 + [pl.BlockSpec((1, 1, PAGE_SIZE, KV_LORA), page_map(g)) for g in range(group)]
                 + [pl.BlockSpec((1, 1, MLA_ROPE, PAGE_SIZE), page_map(g)) for g in range(group)],
        out_specs=pl.BlockSpec((MLA_HEADS, t, KV_LORA), q_map),
        scratch_shapes=[pltpu.VMEM((rows, 1), F32), pltpu.VMEM((rows, 1), F32), pltpu.VMEM((rows, KV_LORA), F32)],
    )
    return pl.pallas_call(
        functools.partial(_mla_decode_body, t=t, group=group),
        grid_spec=grid_spec,
        out_shape=jax.ShapeDtypeStruct((MLA_HEADS, bsz * t, KV_LORA), F32),
        compiler_params=_params("parallel", "arbitrary"),
        name="mla_decode",
    )(page_table, q_cat, ckv, kpe, *([cache_kv] * group), *([cache_pe_t] * group))


def _tri_ge(n):
    j = lax.broadcasted_iota(jnp.int32, (n, n), 0)
    s = lax.broadcasted_iota(jnp.int32, (n, n), 1)
    return (j >= s).astype(BF16)


def _rev_cumsum(x, tri):
    hi, lo = _split_bf16(x)
    return _dot(hi, tri) + _dot(lo, tri)


def _sb_prompt_body(q_ref, k_ref, v_ref, o_ref, carry_sc, acc_sc, z_sc, lk_sc, *, tq, tk):
    qi = pl.program_id(1)
    kk = pl.program_id(2)
    last = ((qi + 1) * tq - 1) // tk
    ki = last - kk

    @pl.when(kk == 0)
    def _():
        carry_sc[...] = jnp.zeros(carry_sc.shape, F32)
        acc_sc[...] = jnp.zeros(acc_sc.shape, F32)

    def step(masked):
        q = q_ref[...].astype(BF16)
        k = k_ref[...].astype(BF16)
        v = v_ref[...].astype(BF16)
        if masked:
            q_pos = qi * tq + lax.broadcasted_iota(jnp.int32, (tq, tk), 0)
            k_pos = ki * tk + lax.broadcasted_iota(jnp.int32, (tq, tk), 1)
            mask = k_pos < q_pos
        for h in range(SB_HEADS):
            sl = slice(h * SB_HEAD_DIM, (h + 1) * SB_HEAD_DIM)
            z = _dot_nt(q[:, sl], k[:, sl]) * SB_SCALE
            lk = _neg_softplus(z)
            if masked:
                lk = jnp.where(mask, lk, 0.0)
            z_sc[h] = z
            lk_sc[h] = lk
        cum = _rev_cumsum(lk_sc[...].reshape(SB_HEADS * tq, tk), _tri_ge(tk)).reshape(SB_HEADS, tq, tk)
        for h in range(SB_HEADS):
            sl = slice(h * SB_HEAD_DIM, (h + 1) * SB_HEAD_DIM)
            cum_h = cum[h]
            w = jnp.exp(z_sc[h] + carry_sc[h] + cum_h)
            if masked:
                w = jnp.where(mask, w, 0.0)
            acc_sc[h] += _dot(w.astype(BF16), v[:, sl])
            carry_sc[h] += cum_h[:, :1]

    crosses_diag = (ki + 1) * tk > qi * tq
    pl.when(jnp.logical_and(kk <= last, crosses_diag))(lambda: step(True))
    pl.when(jnp.logical_and(kk <= last, jnp.logical_not(crosses_diag)))(lambda: step(False))

    @pl.when(kk == last)
    def _():
        o_ref[...] = jnp.concatenate([acc_sc[h] for h in range(SB_HEADS)], axis=1)


def sb_prompt(p2, bsz, t):
    tq = min(256, t)
    tk = min(256, t)
    nq, nk = t // tq, t // tk

    def kv_map(col):
        def f(b, qi, kk):
            last = ((qi + 1) * tq - 1) // tk
            return (b * nk + jnp.maximum(last - kk, 0), col)
        return f

    return pl.pallas_call(
        functools.partial(_sb_prompt_body, tq=tq, tk=tk),
        grid=(bsz, nq, nk),
        in_specs=[pl.BlockSpec((tq, BRANCH_WIDTH), lambda b, qi, kk: (b * nq + qi, P2_Q_SB)),
                  pl.BlockSpec((tk, BRANCH_WIDTH), kv_map(P2_K_SB)),
                  pl.BlockSpec((tk, BRANCH_WIDTH), kv_map(P2_V_SB))],
        out_specs=pl.BlockSpec((tq, BRANCH_WIDTH), lambda b, qi, kk: (b * nq + qi, 0)),
        out_shape=jax.ShapeDtypeStruct((bsz * t, BRANCH_WIDTH), F32),
        scratch_shapes=[pltpu.VMEM((SB_HEADS, tq, 1), F32),
                        pltpu.VMEM((SB_HEADS, tq, SB_HEAD_DIM), F32),
                        pltpu.VMEM((SB_HEADS, tq, tk), F32),
                        pltpu.VMEM((SB_HEADS, tq, tk), F32)],
        compiler_params=_params("parallel", "parallel", "arbitrary"),
        name="sb_prompt",
    )(p2, p2, p2)


def _sb_decode_body(pt_ref, q_ref, knew_ref, vnew_ref, *rest, t, group):
    k_refs = rest[:group]
    v_refs = rest[group:2 * group]
    o_ref, carry_sc, acc_sc = rest[2 * group:]
    j = pl.program_id(1)
    rows = SB_HEADS * t
    width = BRANCH_WIDTH
    head_of_row = lax.broadcasted_iota(jnp.int32, (rows, width), 0) // t
    head_of_col = lax.broadcasted_iota(jnp.int32, (rows, width), 1) // SB_HEAD_DIM
    own_head = head_of_row == head_of_col
    q = q_ref[...]
    q_bd = jnp.where(own_head, jnp.concatenate([q] * SB_HEADS, axis=0), 0.0).astype(BF16)

    def weights(z, mask, carry):
        n = z.shape[1]
        ch = min(n, 2 * LANES)
        tri = _tri_ge(ch)
        lk = _neg_softplus(z)
        if mask is not None:
            lk = jnp.where(mask, lk, 0.0)
        ws = [None] * (n // ch)
        for i in reversed(range(n // ch)):
            cum = _rev_cumsum(lk[:, i * ch:(i + 1) * ch], tri)
            ws[i] = jnp.exp(z[:, i * ch:(i + 1) * ch] + carry + cum)
            carry = carry + cum[:, :1]
        w = ws[0] if len(ws) == 1 else jnp.concatenate(ws, axis=1)
        if mask is not None:
            w = jnp.where(mask, w, 0.0)
        return w.astype(BF16), carry

    @pl.when(j == 0)
    def _():
        pad = jnp.zeros((LANES - t, width), F32)
        k = jnp.concatenate([knew_ref[...], pad], axis=0).astype(BF16)
        v = jnp.concatenate([vnew_ref[...], pad], axis=0).astype(BF16)
        q_t = lax.broadcasted_iota(jnp.int32, (rows, LANES), 0) % t
        k_t = lax.broadcasted_iota(jnp.int32, (rows, LANES), 1)
        w, carry = weights(_dot_nt(q_bd, k) * SB_SCALE, k_t < q_t, jnp.zeros((rows, 1), F32))
        carry_sc[...] = carry
        acc_sc[...] = _dot(w, v)

    kt = jnp.concatenate([k_refs[g][0, 0].reshape(width, PAGE_SIZE).astype(BF16) for g in range(group)], axis=1)
    vt = jnp.concatenate([v_refs[g][0, 0].reshape(width, PAGE_SIZE).astype(BF16) for g in range(group)], axis=1)
    w, carry = weights(_dot(q_bd, kt) * SB_SCALE, None, carry_sc[...])
    acc_sc[...] += _dot_nt(w, vt)
    carry_sc[...] = carry

    @pl.when(j == pl.num_programs(1) - 1)
    def _():
        a = jnp.where(own_head, acc_sc[...], 0.0)
        out = a[0:t]
        for h in range(1, SB_HEADS):
            out = out + a[h * t:(h + 1) * t]
        o_ref[...] = out


def sb_decode(layer, page_table, p2, cache_kt, cache_vt, bsz, t):
    n_pages = page_table.shape[1]
    group = _page_group(n_pages, 8)
    steps = n_pages // group
    rows = SB_HEADS * t
    page_block = (1, 1, SB_HEADS, SB_HEAD_DIM, PAGE_SIZE)

    def page_map(g):
        return lambda b, j, pt: (layer, pt[b, (steps - 1 - j) * group + g], 0, 0, 0)

    def col_map(col):
        return lambda b, j, pt: (b, col)

    grid_spec = pltpu.PrefetchScalarGridSpec(
        num_scalar_prefetch=1,
        grid=(bsz, steps),
        in_specs=[pl.BlockSpec((t, BRANCH_WIDTH), col_map(P2_Q_SB)),
                  pl.BlockSpec((t, BRANCH_WIDTH), col_map(P2_K_SB)),
                  pl.BlockSpec((t, BRANCH_WIDTH), col_map(P2_V_SB))]
                 + [pl.BlockSpec(page_block, page_map(g)) for g in range(group)]
                 + [pl.BlockSpec(page_block, page_map(g)) for g in range(group)],
        out_specs=pl.BlockSpec((t, BRANCH_WIDTH), col_map(0)),
        scratch_shapes=[pltpu.VMEM((rows, 1), F32), pltpu.VMEM((rows, BRANCH_WIDTH), F32)],
    )
    return pl.pallas_call(
        functools.partial(_sb_decode_body, t=t, group=group),
        grid_spec=grid_spec,
        out_shape=jax.ShapeDtypeStruct((bsz * t, BRANCH_WIDTH), F32),
        compiler_params=_params("parallel", "arbitrary"),
        name="sb_decode",
    )(page_table, p2, p2, p2, *([cache_kt] * group), *([cache_vt] * group))


def _gelu_tanh(x):
    return 0.5 * x * (1.0 + jnp.tanh(math.sqrt(2.0 / math.pi) * (x + 0.044715 * x * x * x)))


def _s5_body(u_ref, h0re_ref, h0im_ref, lbre_ref, lbim_ref, bbre_ref, bbim_ref, cre_ref, cim_ref, d_ref,
             wglu_ref, bglu_ref, o_ref, hre_ref, him_ref, bure_sc, buim_sc, hre_sc, him_sc, *, steps, r, lane_w):
    c = pl.program_id(1)

    @pl.when(c == 0)
    def _():
        hre_sc[...] = h0re_ref[0]
        him_sc[...] = h0im_ref[0]

    u = u_ref[...]
    ub = u.astype(BF16)
    bure_sc[...] = _dot(ub, bbre_ref[...])
    buim_sc[...] = _dot(ub, bbim_ref[...])

    for lc in range(S5_WIDTH // lane_w):
        ls = pl.ds(lc * lane_w, lane_w)
        lb_re = jnp.broadcast_to(lbre_ref[:, ls], (r, lane_w))
        lb_im = jnp.broadcast_to(lbim_ref[:, ls], (r, lane_w))

        def body(i, carry):
            h_re, h_im = carry
            rs = pl.ds(pl.multiple_of(i * r, r), r)
            n_re = lb_re * h_re - lb_im * h_im + bure_sc[rs, ls]
            n_im = lb_re * h_im + lb_im * h_re + buim_sc[rs, ls]
            bure_sc[rs, ls] = n_re
            buim_sc[rs, ls] = n_im
            return n_re, n_im

        h_re, h_im = lax.fori_loop(0, steps, body, (hre_sc[:, ls], him_sc[:, ls]))
        hre_sc[:, ls] = h_re
        him_sc[:, ls] = h_im

    y = (_dot(bure_sc[...].astype(BF16), cre_ref[...]) - _dot(buim_sc[...].astype(BF16), cim_ref[...])
         + d_ref[...] * u)
    zz = _dot(_gelu_tanh(y).astype(BF16), wglu_ref[...]) + bglu_ref[...]
    o_ref[...] = zz[:, :BRANCH_WIDTH] * jax.nn.sigmoid(zz[:, BRANCH_WIDTH:])

    @pl.when(c == pl.num_programs(1) - 1)
    def _():
        hre_ref[0] = hre_sc[...]
        him_ref[0] = him_sc[...]


def s5_branch(u_src, u_col, h0_re, h0_im, lw, nblk, nchunk, steps, r, lane_w):
    rows = steps * r
    const = lambda b, c: (0, 0)
    blk = lambda b, c: (b, 0, 0)
    return pl.pallas_call(
        functools.partial(_s5_body, steps=steps, r=r, lane_w=lane_w),
        grid=(nblk, nchunk),
        in_specs=[pl.BlockSpec((rows, BRANCH_WIDTH), lambda b, c: (b * nchunk + c, u_col)),
                  pl.BlockSpec((1, r, S5_WIDTH), blk),
                  pl.BlockSpec((1, r, S5_WIDTH), blk),
                  pl.BlockSpec((1, S5_WIDTH), const),
                  pl.BlockSpec((1, S5_WIDTH), const),
                  pl.BlockSpec((BRANCH_WIDTH, S5_WIDTH), const),
                  pl.BlockSpec((BRANCH_WIDTH, S5_WIDTH), const),
                  pl.BlockSpec((S5_WIDTH, BRANCH_WIDTH), const),
                  pl.BlockSpec((S5_WIDTH, BRANCH_WIDTH), const),
                  pl.BlockSpec((1, BRANCH_WIDTH), const),
                  pl.BlockSpec((BRANCH_WIDTH, 2 * BRANCH_WIDTH), const),
                  pl.BlockSpec((1, 2 * BRANCH_WIDTH), const)],
        out_specs=[pl.BlockSpec((rows, BRANCH_WIDTH), lambda b, c: (b * nchunk + c, 0)),
                   pl.BlockSpec((1, r, S5_WIDTH), blk),
                   pl.BlockSpec((1, r, S5_WIDTH), blk)],
        out_shape=[jax.ShapeDtypeStruct((nblk * nchunk * rows, BRANCH_WIDTH), F32),
                   jax.ShapeDtypeStruct((nblk, r, S5_WIDTH), F32),
                   jax.ShapeDtypeStruct((nblk, r, S5_WIDTH), F32)],
        scratch_shapes=[pltpu.VMEM((rows, S5_WIDTH), F32), pltpu.VMEM((rows, S5_WIDTH), F32),
                        pltpu.VMEM((r, S5_WIDTH), F32), pltpu.VMEM((r, S5_WIDTH), F32)],
        compiler_params=_params("parallel", "arbitrary"),
        name="s5_branch",
    )(u_src, h0_re, h0_im, lw["lb_re"], lw["lb_im"], lw["bb_re"], lw["bb_im"], lw["c_re"], lw["c_im"],
      lw["s5_d"], lw["w_glu"], lw["b_glu"])


def _xattn_body(q_ref, k_ref, v_ref, o_ref, *, nb):
    for i in range(nb):
        q = q_ref[i].astype(BF16)
        k = k_ref[i].astype(BF16)
        v = v_ref[i].astype(BF16)
        outs = []
        for h in range(X_HEADS):
            sl = slice(h * X_HEAD_DIM, (h + 1) * X_HEAD_DIM)
            s = _dot_nt(q[:, sl], k[:, sl]) * X_SCALE
            p = jnp.exp(s - jnp.max(s, axis=-1, keepdims=True))
            p = p / jnp.sum(p, axis=-1, keepdims=True)
            outs.append(_dot(p.astype(BF16), v[:, sl]))
        o_ref[i] = jnp.concatenate(outs, axis=1)


def cross_attend(p2, mem_k, mem_v, bsz, t):
    n_mem = mem_k.shape[1]
    tq = min(512, t)
    nb = 8 if (t <= 8 and bsz % 8 == 0) else 1
    p3 = p2.reshape(bsz, t, P2_WIDTH)
    mem_map = lambda b, qi: (b, 0, 0)
    out = pl.pallas_call(
        functools.partial(_xattn_body, nb=nb),
        grid=(bsz // nb, t // tq),
        in_specs=[pl.BlockSpec((nb, tq, BRANCH_WIDTH), lambda b, qi: (b, qi, P2_Q_X)),
                  pl.BlockSpec((nb, n_mem, BRANCH_WIDTH), mem_map),
                  pl.BlockSpec((nb, n_mem, BRANCH_WIDTH), mem_map)],
        out_specs=pl.BlockSpec((nb, tq, BRANCH_WIDTH), lambda b, qi: (b, qi, 0)),
        out_shape=jax.ShapeDtypeStruct((bsz, t, BRANCH_WIDTH), F32),
        compiler_params=_params("parallel", "parallel"),
        name="cross_attend",
    )(p3, mem_k, mem_v)
    return out.reshape(bsz * t, BRANCH_WIDTH)


def _merge_body(x_ref, olat_ref, os5_ref, osb_ref, ox_ref, gmla_ref, gs5_ref, gsb_ref, gx_ref, gm_ref,
                wuv_ref, wbr_ref, wout_ref, gfin_ref, o_ref, *, final):
    o_mla = _dot(olat_ref[0].astype(BF16), wuv_ref[0])
    for h in range(1, MLA_HEADS):
        o_mla = o_mla + _dot(olat_ref[h].astype(BF16), wuv_ref[h])
    branches = ((o_mla, gmla_ref), (os5_ref[...], gs5_ref), (osb_ref[...], gsb_ref), (ox_ref[...], gx_ref))
    merged = None
    for n, (o, g_ref) in enumerate(branches):
        br = (o * jax.nn.silu(g_ref[...])).astype(BF16)
        proj = _dot(br, wbr_ref[n])
        term = jax.nn.sigmoid(gm_ref[:, n * D_MODEL:(n + 1) * D_MODEL]) * proj
        merged = term if merged is None else merged + term
    x = x_ref[...] + _dot(merged.astype(BF16), wout_ref[...])
    o_ref[...] = _rms(x, gfin_ref[...]) if final else x


def merge_out(x, o_lat, o_s5, o_sb, o_x, p2, lw, g_final, final):
    m, d = x.shape
    tm = min(256, m)
    row = lambda i: (i, 0)
    const2 = lambda i: (0, 0)
    const3 = lambda i: (0, 0, 0)
    col = lambda c: (lambda i: (i, c))
    bw = pl.BlockSpec((tm, BRANCH_WIDTH), row)
    return pl.pallas_call(
        functools.partial(_merge_body, final=final),
        grid=(m // tm,),
        in_specs=[pl.BlockSpec((tm, d), row),
                  pl.BlockSpec((MLA_HEADS, tm, KV_LORA), lambda i: (0, i, 0)),
                  bw, bw, bw,
                  pl.BlockSpec((tm, BRANCH_WIDTH), col(P2_G_MLA)),
                  pl.BlockSpec((tm, BRANCH_WIDTH), col(P2_G_S5)),
                  pl.BlockSpec((tm, BRANCH_WIDTH), col(P2_G_SB)),
                  pl.BlockSpec((tm, BRANCH_WIDTH), col(P2_G_X)),
                  pl.BlockSpec((tm, P2_GMERGE_W), col(0)),
                  pl.BlockSpec(lw["wuv_pad"].shape, const3),
                  pl.BlockSpec(lw["w_branch"].shape, const3),
                  pl.BlockSpec(lw["w_out"].shape, const2),
                  pl.BlockSpec((1, d), const2)],
        out_specs=pl.BlockSpec((tm, d), row),
        out_shape=jax.ShapeDtypeStruct((m, d), F32),
        compiler_params=_params("parallel"),
        name="merge_out",
    )(x, o_lat, o_s5, o_sb, o_x, p2, p2, p2, p2, p2, lw["wuv_pad"], lw["w_branch"], lw["w_out"], g_final)


def _block_diag(blocks):
    g, r, c = blocks.shape
    eye = jnp.eye(g, dtype=blocks.dtype)
    return (eye[:, None, :, None] * blocks[:, :, None, :]).reshape(g * r, g * c)


def _layer_weights(l, g_norm, w_in, g_q_lat, g_kv_lat, w_uq, w_uk, w_uv, s5_a_re, s5_a_im, s5_log_dt,
                   s5_b_re, s5_b_im, s5_c_re, s5_c_im, s5_d, w_glu, b_glu, w_branch, w_out):
    w = w_in[l]
    o1 = Q_LORA + KV_LORA
    o2 = o1 + MLA_ROPE
    o3 = o2 + 9 * BRANCH_WIDTH
    rope_pad = jnp.zeros((D_MODEL, ROPE_PAD - MLA_ROPE), F32)
    w1 = jnp.concatenate([w[:, :o2], rope_pad], axis=1).astype(BF16)
    w2 = jnp.concatenate([w[:, o3:], w[:, o2:o3]], axis=1).astype(BF16)
    hd = MLA_NOPE + MLA_ROPE
    wq = w_uq[l].reshape(Q_LORA, MLA_HEADS, hd)
    wq_nope = wq[:, :, :MLA_NOPE].reshape(Q_LORA, MLA_HEADS * MLA_NOPE).astype(BF16)
    wq_pe = jnp.pad(wq[:, :, MLA_NOPE:], ((0, 0), (0, 0), (0, ROPE_PAD - MLA_ROPE))).reshape(
        Q_LORA, MLA_HEADS * ROPE_PAD).astype(BF16)
    wuk_bd = _block_diag(jnp.transpose(w_uk[l], (1, 2, 0))).astype(BF16)
    wuv = jnp.transpose(w_uv[l], (1, 0, 2))
    eye_h = jnp.eye(MLA_HEADS, dtype=F32)
    wuv_pad = (wuv[:, :, None, :] * eye_h[:, None, :, None]).reshape(MLA_HEADS, KV_LORA, BRANCH_WIDTH).astype(BF16)
    a_re, a_im = s5_a_re[l].astype(F32), s5_a_im[l].astype(F32)
    dt = jnp.exp(s5_log_dt[l].astype(F32))[:, None]
    mag = jnp.exp(a_re * dt)
    lb_re, lb_im = mag * jnp.cos(a_im * dt), mag * jnp.sin(a_im * dt)
    den = a_re * a_re + a_im * a_im
    n_re = lb_re - 1.0
    f_re = (n_re * a_re + lb_im * a_im) / den
    f_im = (lb_im * a_re - n_re * a_im) / den
    b_re, b_im = s5_b_re[l].astype(F32), s5_b_im[l].astype(F32)
    bb_re = f_re[..., None] * b_re - f_im[..., None] * b_im
    bb_im = f_re[..., None] * b_im + f_im[..., None] * b_re
    return dict(
        g_norm=g_norm[l].reshape(1, D_MODEL), w1=w1, w2=w2,
        g_q_lat=g_q_lat[l].reshape(1, Q_LORA), g_kv_lat=g_kv_lat[l].reshape(1, KV_LORA),
        wq_nope=wq_nope, wq_pe=wq_pe, wuk_bd=wuk_bd, wuv_pad=wuv_pad,
        lb_re=lb_re.reshape(1, S5_WIDTH), lb_im=lb_im.reshape(1, S5_WIDTH),
        bb_re=_block_diag(jnp.transpose(bb_re, (0, 2, 1))).astype(BF16),
        bb_im=_block_diag(jnp.transpose(bb_im, (0, 2, 1))).astype(BF16),
        c_re=_block_diag(jnp.transpose(s5_c_re[l].astype(F32), (0, 2, 1))).astype(BF16),
        c_im=_block_diag(jnp.transpose(s5_c_im[l].astype(F32), (0, 2, 1))).astype(BF16),
        s5_d=s5_d[l].reshape(1, BRANCH_WIDTH).astype(F32),
        w_glu=w_glu[l].astype(BF16), b_glu=b_glu[l].reshape(1, 2 * BRANCH_WIDTH),
        w_branch=w_branch[l].astype(BF16), w_out=w_out[l].astype(BF16))


def _rope_tables(pos):
    half = MLA_ROPE // 2
    freq = ROPE_BASE ** (-jnp.arange(half, dtype=F32) / half)
    ang = pos.astype(F32)[:, None] * freq
    cos, sin = jnp.cos(ang), jnp.sin(ang)
    pad = jnp.zeros((pos.shape[0], ROPE_PAD - MLA_ROPE), F32)
    return jnp.concatenate([cos, cos, pad], axis=1), jnp.concatenate([-sin, sin, pad], axis=1)


def _branches(x, lw, cos_t, sin_t, q_dtype):
    p2 = norm_matmul(x, lw["g_norm"], lw["w2"], tn=P2_WIDTH // 4)
    q_cat, ckv, kpe, k_cat = mla_prep(x, lw, cos_t, sin_t, q_dtype)
    return p2, q_cat, ckv, kpe, k_cat


def kernel(x_prompt, x_sample, mem_prompt, cache_mla_kv, cache_mla_pe, cache_sb_k, cache_sb_v, state_s5_re, state_s5_im, cache_mem_k, cache_mem_v, page_table, g_norm, w_in, g_q_lat, g_kv_lat, w_uq, w_uk, w_uv, s5_a_re, s5_a_im, s5_log_dt, s5_b_re, s5_b_im, s5_c_re, s5_c_im, s5_d, w_glu, b_glu, g_mem, w_mem_kv, w_branch, w_out, g_final):
    depth = w_in.shape[0]
    bp, tp, d = x_prompt.shape
    bs, ts, _ = x_sample.shape
    n_mem = mem_prompt.shape[1]
    n_pool = cache_mla_kv.shape[1]
    past_len = page_table.shape[1] * PAGE_SIZE
    mp, ms = bp * tp, bs * ts

    cos_p, sin_p = _rope_tables(jnp.arange(tp, dtype=jnp.int32))
    cos_p, sin_p = jnp.tile(cos_p, (bp, 1)), jnp.tile(sin_p, (bp, 1))
    cos_s, sin_s = _rope_tables(past_len + jnp.arange(ts, dtype=jnp.int32))
    cos_s, sin_s = jnp.tile(cos_s, (bs, 1)), jnp.tile(sin_s, (bs, 1))

    cache_kt = jnp.transpose(cache_sb_k, (0, 1, 3, 4, 2))
    cache_vt = jnp.transpose(cache_sb_v, (0, 1, 3, 4, 2))
    cache_pe_t = jnp.transpose(cache_mla_pe, (0, 1, 3, 2))
    page_table = page_table.astype(jnp.int32)
    g_fin = g_final.reshape(1, d)
    zeros_p = jnp.zeros((bp, 1, S5_WIDTH), F32)

    r_s = 32 if bs % 32 == 0 else bs
    nblk_s = bs // r_s
    chunk_p = min(256, tp)

    xp = x_prompt.reshape(mp, d)
    xs = x_sample.reshape(ms, d)
    mem = mem_prompt.reshape(bp * n_mem, d)
    new_p = [[] for _ in range(8)]
    new_s = [[] for _ in range(6)]
    for l in range(depth):
        lw = _layer_weights(l, g_norm, w_in, g_q_lat, g_kv_lat, w_uq, w_uk, w_uv, s5_a_re, s5_a_im, s5_log_dt,
                            s5_b_re, s5_b_im, s5_c_re, s5_c_im, s5_d, w_glu, b_glu, w_branch, w_out)
        final = l == depth - 1
        mkv = norm_matmul(mem, g_mem[l], w_mem_kv[l].astype(BF16), tn=2 * BRANCH_WIDTH)
        mk = mkv[:, :BRANCH_WIDTH].reshape(bp, n_mem, BRANCH_WIDTH)
        mv = mkv[:, BRANCH_WIDTH:].reshape(bp, n_mem, BRANCH_WIDTH)
        p2, q_cat, ckv, kpe, k_cat = _branches(xp, lw, cos_p, sin_p, BF16)
        o_lat = mla_prompt(q_cat, k_cat, bp, tp)
        o_sb = sb_prompt(p2, bp, tp)
        o_s5, h_re, h_im = s5_branch(p2, P2_U_S5, zeros_p, zeros_p, lw, bp, tp // chunk_p, chunk_p, 1,
                                     S5_WIDTH // 2)
        o_x = cross_attend(p2, mk, mv, bp, tp)
        xp = merge_out(xp, o_lat, o_s5, o_sb, o_x, p2, lw, g_fin, final)
        st_p = (ckv.reshape(bp, tp, KV_LORA), kpe[:, :MLA_ROPE].reshape(bp, tp, MLA_ROPE),
                p2[:, P2_K_SB * BRANCH_WIDTH:(P2_K_SB + 1) * BRANCH_WIDTH].reshape(bp, tp, SB_HEADS, SB_HEAD_DIM),
                p2[:, P2_V_SB * BRANCH_WIDTH:(P2_V_SB + 1) * BRANCH_WIDTH].reshape(bp, tp, SB_HEADS, SB_HEAD_DIM),
                h_re.reshape(bp, S5_GROUPS, S5_STATE), h_im.reshape(bp, S5_GROUPS, S5_STATE),
                mk.reshape(bp, n_mem, X_HEADS, X_HEAD_DIM), mv.reshape(bp, n_mem, X_HEADS, X_HEAD_DIM))
        p2, q_cat, ckv, kpe, _ = _branches(xs, lw, cos_s, sin_s, F32)
        o_lat = mla_decode(l, page_table, q_cat, ckv, kpe, cache_mla_kv, cache_pe_t, bs, ts)
        o_sb = sb_decode(l, page_table, p2, cache_kt, cache_vt, bs, ts)
        u_s = p2[:, P2_U_S5 * BRANCH_WIDTH:(P2_U_S5 + 1) * BRANCH_WIDTH]
        u_s = u_s.reshape(nblk_s, r_s, ts, BRANCH_WIDTH).transpose(0, 2, 1, 3).reshape(ms, BRANCH_WIDTH)
        o_s5, h_re, h_im = s5_branch(u_s, 0, state_s5_re[l].reshape(nblk_s, r_s, S5_WIDTH),
                                     state_s5_im[l].reshape(nblk_s, r_s, S5_WIDTH), lw, nblk_s, 1, ts, r_s,
                                     min(512, S5_WIDTH))
        o_s5 = o_s5.reshape(nblk_s, ts, r_s, BRANCH_WIDTH).transpose(0, 2, 1, 3).reshape(ms, BRANCH_WIDTH)
        o_x = cross_attend(p2, cache_mem_k[l].reshape(bs, n_mem, BRANCH_WIDTH),
                           cache_mem_v[l].reshape(bs, n_mem, BRANCH_WIDTH), bs, ts)
        xs = merge_out(xs, o_lat, o_s5, o_sb, o_x, p2, lw, g_fin, final)
        st_s = (ckv.reshape(bs, ts, KV_LORA), kpe[:, :MLA_ROPE].reshape(bs, ts, MLA_ROPE),
                p2[:, P2_K_SB * BRANCH_WIDTH:(P2_K_SB + 1) * BRANCH_WIDTH].reshape(bs, ts, SB_HEADS, SB_HEAD_DIM),
                p2[:, P2_V_SB * BRANCH_WIDTH:(P2_V_SB + 1) * BRANCH_WIDTH].reshape(bs, ts, SB_HEADS, SB_HEAD_DIM),
                h_re.reshape(bs, S5_GROUPS, S5_STATE), h_im.reshape(bs, S5_GROUPS, S5_STATE))
        for lst, a in zip(new_p, st_p):
            lst.append(a)
        for lst, a in zip(new_s, st_s):
            lst.append(a)
    outs_p = [jnp.stack(a) for a in new_p]
    outs_s = [jnp.stack(a) for a in new_s]
    return (xp.reshape(bp, tp, d), xs.reshape(bs, ts, d), *outs_p, *outs_s)
```

```python
import functools
import math

import jax
import jax.numpy as jnp
from jax import lax
from jax.experimental import pallas as pl
from jax.experimental.pallas import tpu as pltpu

F32 = jnp.float32
BF16 = jnp.bfloat16

D_MODEL = 1024
BRANCH_WIDTH = D_MODEL // 2
N_BRANCH = 4
MLA_HEADS = 8
MLA_NOPE = 64
MLA_ROPE = 32
MLA_V_DIM = BRANCH_WIDTH // MLA_HEADS
Q_LORA = 3 * D_MODEL // 8
KV_LORA = D_MODEL // 4
MLA_SCALE = (MLA_NOPE + MLA_ROPE) ** -0.5
ROPE_BASE = 10000.0
S5_GROUP = 16
S5_GROUPS = BRANCH_WIDTH // S5_GROUP
S5_STATE = 64
S5_WIDTH = S5_GROUPS * S5_STATE
SB_HEADS = 8
SB_HEAD_DIM = BRANCH_WIDTH // SB_HEADS
SB_SCALE = SB_HEAD_DIM ** -0.5
X_HEADS = 4
X_HEAD_DIM = BRANCH_WIDTH // X_HEADS
X_SCALE = X_HEAD_DIM ** -0.5
RMS_EPS = 1e-6
PAGE_SIZE = 128

LANES = 128
SUBLANES = 8
ROPE_PAD = LANES
QK_WIDTH = KV_LORA + ROPE_PAD
LOG2E = math.log2(math.e)
MLA_DECODE_SPLIT = 1
DECODE_PAGE_GROUP = 8
CUMSUM_PASSES = 1
VMEM_LIMIT = 56 * 1024 * 1024

P2_GMERGE_W = N_BRANCH * D_MODEL
_P2_BASE = P2_GMERGE_W // BRANCH_WIDTH
P2_G_MLA, P2_U_S5, P2_G_S5, P2_Q_SB, P2_K_SB, P2_V_SB, P2_G_SB, P2_Q_X, P2_G_X = (
    _P2_BASE + i for i in range(9))
P2_WIDTH = P2_GMERGE_W + 9 * BRANCH_WIDTH


def _params(*sem):
    return pltpu.CompilerParams(dimension_semantics=sem, vmem_limit_bytes=VMEM_LIMIT)


def _rms(x, g):
    return x * lax.rsqrt(jnp.mean(x * x, axis=-1, keepdims=True) + RMS_EPS) * g


def _dot(a, b):
    return jnp.dot(a, b, preferred_element_type=F32)


def _dot_nt(a, b):
    return lax.dot_general(a, b, (((1,), (1,)), ((), ())), preferred_element_type=F32)


def _split_bf16(x):
    hi = x.astype(BF16)
    lo = (x - hi.astype(F32)).astype(BF16)
    return hi, lo


def _softplus(z):
    return jnp.maximum(z, 0.0) + jnp.log(1.0 + jnp.exp(-jnp.abs(z)))


def _norm_mm_body(x_ref, g_ref, w_ref, o_ref):
    h = _rms(x_ref[...], g_ref[...]).astype(BF16)
    o_ref[...] = _dot(h, w_ref[...])


def norm_matmul(x, g, w, tn):
    m, d = x.shape
    n = w.shape[1]
    tm = min(512, m)
    return pl.pallas_call(
        _norm_mm_body,
        grid=(n // tn, m // tm),
        in_specs=[pl.BlockSpec((tm, d), lambda j, i: (i, 0)),
                  pl.BlockSpec((1, d), lambda j, i: (0, 0)),
                  pl.BlockSpec((d, tn), lambda j, i: (0, j))],
        out_specs=pl.BlockSpec((tm, tn), lambda j, i: (i, j)),
        out_shape=jax.ShapeDtypeStruct((m, n), F32),
        compiler_params=_params("parallel", "parallel"),
        name="norm_matmul",
    )(x, g.reshape(1, d), w)


def _rope_padded(x, cos_t, sin_t):
    n = x.shape[-1]
    half = MLA_ROPE // 2
    lane = lax.broadcasted_iota(jnp.int32, x.shape, 1) % ROPE_PAD
    swapped = jnp.where(lane < half, pltpu.roll(x, n - half, 1), pltpu.roll(x, half, 1))
    reps = n // ROPE_PAD
    if reps > 1:
        cos_t = jnp.concatenate([cos_t] * reps, axis=1)
        sin_t = jnp.concatenate([sin_t] * reps, axis=1)
    return x * cos_t + swapped * sin_t


def _mla_prep_body(x_ref, gn_ref, w1_ref, gq_ref, gkv_ref, wqn_ref, wqp_ref, wuk_ref, cos_ref, sin_ref,
                   qcat_ref, ckv_ref, kpe_ref, kcat_ref):
    h = _rms(x_ref[...], gn_ref[...]).astype(BF16)
    c = _dot(h, w1_ref[...])
    cq = c[:, :Q_LORA]
    ckv_raw = c[:, Q_LORA:Q_LORA + KV_LORA]
    kpe_raw = c[:, Q_LORA + KV_LORA:]
    cos_t = cos_ref[...]
    sin_t = sin_ref[...]
    ckv = _rms(ckv_raw, gkv_ref[...])
    kpe = _rope_padded(kpe_raw, cos_t, sin_t)
    ckv_ref[...] = ckv
    kpe_ref[...] = kpe
    kcat_ref[...] = jnp.concatenate([ckv, kpe], axis=1).astype(BF16)
    qn = _rms(cq, gq_ref[...]).astype(BF16)
    q_nope = _dot(qn, wqn_ref[...])
    q_pe = _rope_padded(_dot(qn, wqp_ref[...]), cos_t, sin_t)
    q_lat = _dot(q_nope.astype(BF16), wuk_ref[...])
    for hh in range(MLA_HEADS):
        qcat_ref[hh] = jnp.concatenate([q_lat[:, hh * KV_LORA:(hh + 1) * KV_LORA],
                                        q_pe[:, hh * ROPE_PAD:(hh + 1) * ROPE_PAD]], axis=1).astype(qcat_ref.dtype)


def mla_prep(x, lw, cos_t, sin_t, q_dtype):
    m, d = x.shape
    tm = min(256, m)
    row = lambda i: (i, 0)
    const = lambda i: (0, 0)
    w1 = lw["w1"]
    return pl.pallas_call(
        _mla_prep_body,
        grid=(m // tm,),
        in_specs=[pl.BlockSpec((tm, d), row),
                  pl.BlockSpec((1, d), const),
                  pl.BlockSpec(w1.shape, const),
                  pl.BlockSpec((1, Q_LORA), const),
                  pl.BlockSpec((1, KV_LORA), const),
                  pl.BlockSpec(lw["wq_nope"].shape, const),
                  pl.BlockSpec(lw["wq_pe"].shape, const),
                  pl.BlockSpec(lw["wuk_bd"].shape, const),
                  pl.BlockSpec((tm, ROPE_PAD), row),
                  pl.BlockSpec((tm, ROPE_PAD), row)],
        out_specs=[pl.BlockSpec((MLA_HEADS, tm, QK_WIDTH), lambda i: (0, i, 0)),
                   pl.BlockSpec((tm, KV_LORA), row),
                   pl.BlockSpec((tm, ROPE_PAD), row),
                   pl.BlockSpec((tm, QK_WIDTH), row)],
        out_shape=[jax.ShapeDtypeStruct((MLA_HEADS, m, QK_WIDTH), q_dtype),
                   jax.ShapeDtypeStruct((m, KV_LORA), F32),
                   jax.ShapeDtypeStruct((m, ROPE_PAD), F32),
                   jax.ShapeDtypeStruct((m, QK_WIDTH), BF16)],
        compiler_params=_params("parallel"),
        name="mla_prep",
    )(x, lw["g_norm"], w1, lw["g_q_lat"], lw["g_kv_lat"], lw["wq_nope"], lw["wq_pe"], lw["wuk_bd"], cos_t, sin_t)


def _softmax_step(s, v_bf16, m_sc, l_sc, acc_sc):
    m_prev = m_sc[...]
    m_new = jnp.maximum(m_prev, jnp.max(s, axis=-1, keepdims=True))
    alpha = jnp.exp(m_prev - m_new)
    p = jnp.exp(s - m_new)
    l_sc[...] = alpha * l_sc[...] + jnp.sum(p, axis=-1, keepdims=True)
    acc_sc[...] = alpha * acc_sc[...] + _dot(p.astype(BF16), v_bf16)
    m_sc[...] = m_new


def _mla_prompt_body(q_ref, k_ref, o_ref, m_sc, l_sc, acc_sc, *, tq, tk):
    qi = pl.program_id(1)
    ki = pl.program_id(2)
    last = ((qi + 1) * tq - 1) // tk
    rows = MLA_HEADS * tq
    c = MLA_SCALE * LOG2E

    @pl.when(ki == 0)
    def _():
        m_sc[...] = jnp.full(m_sc.shape, -jnp.inf, F32)
        l_sc[...] = jnp.zeros(l_sc.shape, F32)
        acc_sc[...] = jnp.zeros(acc_sc.shape, F32)

    def step(masked):
        q = q_ref[...].reshape(rows, QK_WIDTH)
        k = k_ref[...]
        s = _dot_nt(q, k)
        if masked:
            q_pos = qi * tq + lax.broadcasted_iota(jnp.int32, (tq, tk), 0)
            k_pos = ki * tk + lax.broadcasted_iota(jnp.int32, (tq, tk), 1)
            s = jnp.where((k_pos <= q_pos)[None], s.reshape(MLA_HEADS, tq, tk), -jnp.inf).reshape(rows, tk)
        m_prev = m_sc[...]
        m_new = jnp.maximum(m_prev, jnp.max(s, axis=-1, keepdims=True))
        alpha = jnp.exp2((m_prev - m_new) * c)
        p = jnp.exp2((s - jnp.concatenate([m_new] * (tk // LANES), axis=1)) * c)
        p_lanes = p[:, :LANES]
        for i in range(1, tk // LANES):
            p_lanes = p_lanes + p[:, i * LANES:(i + 1) * LANES]
        l_sc[...] = alpha * l_sc[...] + p_lanes
        acc_sc[...] = (jnp.concatenate([alpha] * (KV_LORA // LANES), axis=1) * acc_sc[...]
                       + _dot(p.astype(BF16), k[:, :KV_LORA]))
        m_sc[...] = m_new

    crosses_diag = (ki + 1) * tk - 1 > qi * tq
    pl.when(jnp.logical_and(ki <= last, crosses_diag))(lambda: step(True))
    pl.when(jnp.logical_and(ki <= last, jnp.logical_not(crosses_diag)))(lambda: step(False))

    @pl.when(ki == last)
    def _():
        l = jnp.sum(l_sc[...], axis=-1, keepdims=True)
        o_ref[...] = (acc_sc[...] / l).reshape(MLA_HEADS, tq, KV_LORA).astype(o_ref.dtype)


def mla_prompt(q_cat, k_cat, bsz, t):
    tq = min(128, t)
    tk = min(512, t)
    nq, nk = t // tq, t // tk
    rows = MLA_HEADS * tq

    def kv_map(b, qi, ki):
        return (b * nk + jnp.minimum(ki, ((qi + 1) * tq - 1) // tk), 0)

    q_map = lambda b, qi, ki: (0, b * nq + qi, 0)
    return pl.pallas_call(
        functools.partial(_mla_prompt_body, tq=tq, tk=tk),
        grid=(bsz, nq, nk),
        in_specs=[pl.BlockSpec((MLA_HEADS, tq, QK_WIDTH), q_map),
                  pl.BlockSpec((tk, QK_WIDTH), kv_map)],
        out_specs=pl.BlockSpec((MLA_HEADS, tq, KV_LORA), q_map),
        out_shape=jax.ShapeDtypeStruct((MLA_HEADS, bsz * t, KV_LORA), BF16),
        scratch_shapes=[pltpu.VMEM((rows, LANES), F32), pltpu.VMEM((rows, LANES), F32),
                        pltpu.VMEM((rows, KV_LORA), F32)],
        compiler_params=_params("parallel", "parallel", "arbitrary"),
        name="mla_prompt",
    )(q_cat, k_cat)


def _mla_decode_step(q_ref, ckv_new_ref, kpe_new_ref, kv_refs, pe_refs, o_ref, m_sc, l_sc, acc_sc, *, t):
    group = len(kv_refs)
    j = pl.program_id(1)
    rows = MLA_HEADS * t
    q = q_ref[...].reshape(rows, QK_WIDTH).astype(BF16)
    ql = q[:, :KV_LORA]
    qp = q[:, KV_LORA:]

    @pl.when(j == 0)
    def _():
        k = jnp.concatenate([ckv_new_ref[...], jnp.zeros((LANES - t, KV_LORA), F32)], axis=0).astype(BF16)
        kp = jnp.concatenate([kpe_new_ref[...], jnp.zeros((LANES - t, ROPE_PAD), F32)], axis=0).astype(BF16)
        s = (_dot_nt(ql, k) + _dot_nt(qp, kp)) * MLA_SCALE
        q_t = lax.broadcasted_iota(jnp.int32, (rows, LANES), 0) % t
        k_t = lax.broadcasted_iota(jnp.int32, (rows, LANES), 1)
        s = jnp.where(k_t <= q_t, s, -jnp.inf)
        m = jnp.max(s, axis=-1, keepdims=True)
        p = jnp.exp(s - m)
        m_sc[...] = m
        l_sc[...] = jnp.sum(p, axis=-1, keepdims=True)
        acc_sc[...] = _dot(p.astype(BF16), k)

    parts = []
    per = max(group // MLA_DECODE_SPLIT, 1)
    for g0 in range(0, group, per):
        k = jnp.concatenate([kv_refs[g][0, 0].astype(BF16) for g in range(g0, g0 + per)], axis=0)
        kpe_t = jnp.concatenate([pe_refs[g][0, 0].astype(BF16) for g in range(g0, g0 + per)], axis=1)
        s = (_dot_nt(ql, k) + _dot(qp[:, :MLA_ROPE], kpe_t)) * MLA_SCALE
        m = jnp.max(s, axis=-1, keepdims=True)
        p = jnp.exp(s - m)
        parts.append((m, jnp.sum(p, axis=-1, keepdims=True), _dot(p.astype(BF16), k)))
    m_prev = m_sc[...]
    m_new = m_prev
    for m, _, _ in parts:
        m_new = jnp.maximum(m_new, m)
    alpha = jnp.exp(m_prev - m_new)
    l_new = alpha * l_sc[...]
    acc_new = alpha * acc_sc[...]
    for m, l, acc in parts:
        beta = jnp.exp(m - m_new)
        l_new = l_new + beta * l
        acc_new = acc_new + beta * acc
    m_sc[...] = m_new
    l_sc[...] = l_new
    acc_sc[...] = acc_new

    @pl.when(j == pl.num_programs(1) - 1)
    def _():
        o_ref[...] = (acc_sc[...] / l_sc[...]).reshape(MLA_HEADS, t, KV_LORA)


def _page_group(n_pages, largest):
    g = largest
    while n_pages % g:
        g //= 2
    return g


def _tri_ge(n):
    j = lax.broadcasted_iota(jnp.int32, (n, n), 0)
    s = lax.broadcasted_iota(jnp.int32, (n, n), 1)
    return (j >= s).astype(BF16)


def _rev_cumsum(x, tri):
    if CUMSUM_PASSES == 1:
        return _dot(x.astype(BF16), tri)
    hi, lo = _split_bf16(x)
    return _dot(hi, tri) + _dot(lo, tri)


def _sb_prompt_body(q_ref, k_ref, v_ref, o_ref, carry_sc, acc_sc, z_sc, lk_sc, *, tq, tk):
    qi = pl.program_id(1)
    kk = pl.program_id(2)
    last = ((qi + 1) * tq - 1) // tk
    ki = last - kk

    @pl.when(kk == 0)
    def _():
        carry_sc[...] = jnp.zeros(carry_sc.shape, F32)
        acc_sc[...] = jnp.zeros(acc_sc.shape, F32)

    def step(masked):
        q = (q_ref[...] * SB_SCALE).astype(BF16)
        k = k_ref[...].astype(BF16)
        v = v_ref[...].astype(BF16)
        if masked:
            q_pos = qi * tq + lax.broadcasted_iota(jnp.int32, (tq, tk), 0)
            k_pos = ki * tk + lax.broadcasted_iota(jnp.int32, (tq, tk), 1)
            mask = k_pos < q_pos
        for h in range(SB_HEADS):
            sl = slice(h * SB_HEAD_DIM, (h + 1) * SB_HEAD_DIM)
            z = _dot_nt(q[:, sl], k[:, sl])
            sp = _softplus(z)
            if masked:
                sp = jnp.where(mask, sp, 0.0)
            z_sc[h] = z
            lk_sc[h] = sp
        cum = _rev_cumsum(lk_sc[...].reshape(SB_HEADS * tq, tk), _tri_ge(tk)).reshape(SB_HEADS, tq, tk)
        for h in range(SB_HEADS):
            sl = slice(h * SB_HEAD_DIM, (h + 1) * SB_HEAD_DIM)
            cum_h = cum[h]
            w = jnp.exp(z_sc[h] - (carry_sc[h] + cum_h))
            if masked:
                w = jnp.where(mask, w, 0.0)
            acc_sc[h] += _dot(w.astype(BF16), v[:, sl])
            carry_sc[h] += cum_h[:, :1]

    crosses_diag = (ki + 1) * tk > qi * tq
    pl.when(jnp.logical_and(kk <= last, crosses_diag))(lambda: step(True))
    pl.when(jnp.logical_and(kk <= last, jnp.logical_not(crosses_diag)))(lambda: step(False))

    @pl.when(kk == last)
    def _():
        o_ref[...] = jnp.concatenate([acc_sc[h] for h in range(SB_HEADS)], axis=1)


def sb_prompt(p2, bsz, t):
    tq = min(256, t)
    tk = min(256, t)
    nq, nk = t // tq, t // tk

    def kv_map(col):
        def f(b, qi, kk):
            last = ((qi + 1) * tq - 1) // tk
            return (b * nk + jnp.maximum(last - kk, 0), col)
        return f

    return pl.pallas_call(
        functools.partial(_sb_prompt_body, tq=tq, tk=tk),
        grid=(bsz, nq, nk),
        in_specs=[pl.BlockSpec((tq, BRANCH_WIDTH), lambda b, qi, kk: (b * nq + qi, P2_Q_SB)),
                  pl.BlockSpec((tk, BRANCH_WIDTH), kv_map(P2_K_SB)),
                  pl.BlockSpec((tk, BRANCH_WIDTH), kv_map(P2_V_SB))],
        out_specs=pl.BlockSpec((tq, BRANCH_WIDTH), lambda b, qi, kk: (b * nq + qi, 0)),
        out_shape=jax.ShapeDtypeStruct((bsz * t, BRANCH_WIDTH), F32),
        scratch_shapes=[pltpu.VMEM((SB_HEADS, tq, 1), F32),
                        pltpu.VMEM((SB_HEADS, tq, SB_HEAD_DIM), F32),
                        pltpu.VMEM((SB_HEADS, tq, tk), F32),
                        pltpu.VMEM((SB_HEADS, tq, tk), F32)],
        compiler_params=_params("parallel", "parallel", "arbitrary"),
        name="sb_prompt",
    )(p2, p2, p2)


def _sb_decode_step(q_ref, knew_ref, vnew_ref, k_refs, v_refs, o_ref, carry_sc, acc_sc, *, t):
    group = len(k_refs)
    j = pl.program_id(1)
    rows = SB_HEADS * t
    width = BRANCH_WIDTH
    head_of_row = lax.broadcasted_iota(jnp.int32, (rows, width), 0) // t
    head_of_col = lax.broadcasted_iota(jnp.int32, (rows, width), 1) // SB_HEAD_DIM
    own_head = head_of_row == head_of_col
    q = q_ref[...] * SB_SCALE
    q_bd = jnp.where(own_head, jnp.concatenate([q] * SB_HEADS, axis=0), 0.0).astype(BF16)

    def weights(z, mask, carry):
        n = z.shape[1]
        ch = min(n, 2 * LANES)
        tri = _tri_ge(ch)
        sp = _softplus(z)
        if mask is not None:
            sp = jnp.where(mask, sp, 0.0)
        ws = [None] * (n // ch)
        for i in reversed(range(n // ch)):
            cum = _rev_cumsum(sp[:, i * ch:(i + 1) * ch], tri)
            ws[i] = jnp.exp(z[:, i * ch:(i + 1) * ch] - (carry + cum))
            carry = carry + cum[:, :1]
        w = ws[0] if len(ws) == 1 else jnp.concatenate(ws, axis=1)
        if mask is not None:
            w = jnp.where(mask, w, 0.0)
        return w.astype(BF16), carry

    @pl.when(j == 0)
    def _():
        pad = jnp.zeros((LANES - t, width), F32)
        k = jnp.concatenate([knew_ref[...], pad], axis=0).astype(BF16)
        v = jnp.concatenate([vnew_ref[...], pad], axis=0).astype(BF16)
        q_t = lax.broadcasted_iota(jnp.int32, (rows, LANES), 0) % t
        k_t = lax.broadcasted_iota(jnp.int32, (rows, LANES), 1)
        w, carry = weights(_dot_nt(q_bd, k), k_t < q_t, jnp.zeros((rows, 1), F32))
        carry_sc[...] = carry
        acc_sc[...] = _dot(w, v)

    kt = jnp.concatenate([k_refs[g][0, 0].reshape(width, PAGE_SIZE).astype(BF16) for g in range(group)], axis=1)
    vt = jnp.concatenate([v_refs[g][0, 0].reshape(width, PAGE_SIZE).astype(BF16) for g in range(group)], axis=1)
    w, carry = weights(_dot(q_bd, kt), None, carry_sc[...])
    acc_sc[...] += _dot_nt(w, vt)
    carry_sc[...] = carry

    @pl.when(j == pl.num_programs(1) - 1)
    def _():
        a = jnp.where(own_head, acc_sc[...], 0.0)
        out = a[0:t]
        for h in range(1, SB_HEADS):
            out = out + a[h * t:(h + 1) * t]
        o_ref[...] = out


def _decode_attn_body(pt_ref, qcat_ref, ckvn_ref, kpen_ref, qsb_ref, ksbn_ref, vsbn_ref, *rest, t, group):
    kv_refs, pe_refs, kt_refs, vt_refs = (rest[i * group:(i + 1) * group] for i in range(4))
    olat_ref, osb_ref, m_sc, l_sc, acc_sc, carry_sc, sbacc_sc = rest[4 * group:]
    _mla_decode_step(qcat_ref, ckvn_ref, kpen_ref, kv_refs, pe_refs, olat_ref, m_sc, l_sc, acc_sc, t=t)
    _sb_decode_step(qsb_ref, ksbn_ref, vsbn_ref, kt_refs, vt_refs, osb_ref, carry_sc, sbacc_sc, t=t)


def decode_attention(layer, page_table, q_cat, ckv, kpe, p2, cache_kv, cache_pe_t, cache_kt, cache_vt, bsz, t):
    n_pages = page_table.shape[1]
    group = _page_group(n_pages, DECODE_PAGE_GROUP)
    steps = n_pages // group
    sb_block = (1, 1, SB_HEADS, SB_HEAD_DIM, PAGE_SIZE)

    def page_map(g, ndim):
        return lambda b, j, pt: (layer, pt[b, (steps - 1 - j) * group + g]) + (0,) * (ndim - 2)

    def col_map(col):
        return lambda b, j, pt: (b, col)

    q_map = lambda b, j, pt: (0, b, 0)
    grid_spec = pltpu.PrefetchScalarGridSpec(
        num_scalar_prefetch=1,
        grid=(bsz, steps),
        in_specs=[pl.BlockSpec((MLA_HEADS, t, QK_WIDTH), q_map),
                  pl.BlockSpec((t, KV_LORA), col_map(0)),
                  pl.BlockSpec((t, ROPE_PAD), col_map(0)),
                  pl.BlockSpec((t, BRANCH_WIDTH), col_map(P2_Q_SB)),
                  pl.BlockSpec((t, BRANCH_WIDTH), col_map(P2_K_SB)),
                  pl.BlockSpec((t, BRANCH_WIDTH), col_map(P2_V_SB))]
                 + [pl.BlockSpec((1, 1, PAGE_SIZE, KV_LORA), page_map(g, 4)) for g in range(group)]
                 + [pl.BlockSpec((1, 1, MLA_ROPE, PAGE_SIZE), page_map(g, 4)) for g in range(group)]
                 + [pl.BlockSpec(sb_block, page_map(g, 5)) for g in range(group)]
                 + [pl.BlockSpec(sb_block, page_map(g, 5)) for g in range(group)],
        out_specs=[pl.BlockSpec((MLA_HEADS, t, KV_LORA), q_map),
                   pl.BlockSpec((t, BRANCH_WIDTH), col_map(0))],
        scratch_shapes=[pltpu.VMEM((MLA_HEADS * t, 1), F32), pltpu.VMEM((MLA_HEADS * t, 1), F32),
                        pltpu.VMEM((MLA_HEADS * t, KV_LORA), F32),
                        pltpu.VMEM((SB_HEADS * t, 1), F32), pltpu.VMEM((SB_HEADS * t, BRANCH_WIDTH), F32)],
    )
    return pl.pallas_call(
        functools.partial(_decode_attn_body, t=t, group=group),
        grid_spec=grid_spec,
        out_shape=[jax.ShapeDtypeStruct((MLA_HEADS, bsz * t, KV_LORA), F32),
                   jax.ShapeDtypeStruct((bsz * t, BRANCH_WIDTH), F32)],
        compiler_params=_params("parallel", "arbitrary"),
        name="decode_attention",
    )(page_table, q_cat, ckv, kpe, p2, p2, p2, *([cache_kv] * group), *([cache_pe_t] * group),
      *([cache_kt] * group), *([cache_vt] * group))


def _gelu_tanh(x):
    return 0.5 * x * (1.0 + jnp.tanh(math.sqrt(2.0 / math.pi) * (x + 0.044715 * x * x * x)))


def _s5_body(u_ref, h0re_ref, h0im_ref, lbre_ref, lbim_ref, bbre_ref, bbim_ref, cre_ref, cim_ref, d_ref,
             wglu_ref, bglu_ref, o_ref, hre_ref, him_ref, bure_sc, buim_sc, hre_sc, him_sc, *, steps, r, lane_w):
    c = pl.program_id(1)
    sub = max(SUBLANES // r, 1)

    @pl.when(c == 0)
    def _():
        hre_sc[...] = h0re_ref[0]
        him_sc[...] = h0im_ref[0]

    u = u_ref[...]
    ub = u.astype(BF16)
    bure_sc[...] = _dot(ub, bbre_ref[...])
    buim_sc[...] = _dot(ub, bbim_ref[...])

    for lc in range(S5_WIDTH // lane_w):
        ls = pl.ds(lc * lane_w, lane_w)
        lb_re = jnp.broadcast_to(lbre_ref[:, ls], (r, lane_w))
        lb_im = jnp.broadcast_to(lbim_ref[:, ls], (r, lane_w))

        def body(i, carry):
            h_re, h_im = carry
            rs = pl.ds(pl.multiple_of(i * (sub * r), sub * r), sub * r)
            x_re = bure_sc[rs, ls]
            x_im = buim_sc[rs, ls]
            out_re, out_im = [], []
            for s in range(sub):
                n_re = lb_re * h_re - lb_im * h_im + x_re[s * r:(s + 1) * r]
                n_im = lb_re * h_im + lb_im * h_re + x_im[s * r:(s + 1) * r]
                h_re, h_im = n_re, n_im
                out_re.append(n_re)
                out_im.append(n_im)
            bure_sc[rs, ls] = out_re[0] if sub == 1 else jnp.concatenate(out_re, axis=0)
            buim_sc[rs, ls] = out_im[0] if sub == 1 else jnp.concatenate(out_im, axis=0)
            return h_re, h_im

        h_re, h_im = lax.fori_loop(0, steps // sub, body, (hre_sc[:, ls], him_sc[:, ls]))
        hre_sc[:, ls] = h_re
        him_sc[:, ls] = h_im

    y = (_dot(bure_sc[...].astype(BF16), cre_ref[...]) - _dot(buim_sc[...].astype(BF16), cim_ref[...])
         + d_ref[...] * u)
    zz = _dot(_gelu_tanh(y).astype(BF16), wglu_ref[...]) + bglu_ref[...]
    o_ref[...] = zz[:, :BRANCH_WIDTH] * jax.nn.sigmoid(zz[:, BRANCH_WIDTH:])

    @pl.when(c == pl.num_programs(1) - 1)
    def _():
        hre_ref[0] = hre_sc[...]
        him_ref[0] = him_sc[...]


def s5_branch(u_src, u_col, h0_re, h0_im, lw, nblk, nchunk, steps, r, lane_w):
    rows = steps * r
    const = lambda b, c: (0, 0)
    blk = lambda b, c: (b, 0, 0)
    return pl.pallas_call(
        functools.partial(_s5_body, steps=steps, r=r, lane_w=lane_w),
        grid=(nblk, nchunk),
        in_specs=[pl.BlockSpec((rows, BRANCH_WIDTH), lambda b, c: (b * nchunk + c, u_col)),
                  pl.BlockSpec((1, r, S5_WIDTH), blk),
                  pl.BlockSpec((1, r, S5_WIDTH), blk),
                  pl.BlockSpec((1, S5_WIDTH), const),
                  pl.BlockSpec((1, S5_WIDTH), const),
                  pl.BlockSpec((BRANCH_WIDTH, S5_WIDTH), const),
                  pl.BlockSpec((BRANCH_WIDTH, S5_WIDTH), const),
                  pl.BlockSpec((S5_WIDTH, BRANCH_WIDTH), const),
                  pl.BlockSpec((S5_WIDTH, BRANCH_WIDTH), const),
                  pl.BlockSpec((1, BRANCH_WIDTH), const),
                  pl.BlockSpec((BRANCH_WIDTH, 2 * BRANCH_WIDTH), const),
                  pl.BlockSpec((1, 2 * BRANCH_WIDTH), const)],
        out_specs=[pl.BlockSpec((rows, BRANCH_WIDTH), lambda b, c: (b * nchunk + c, 0)),
                   pl.BlockSpec((1, r, S5_WIDTH), blk),
                   pl.BlockSpec((1, r, S5_WIDTH), blk)],
        out_shape=[jax.ShapeDtypeStruct((nblk * nchunk * rows, BRANCH_WIDTH), F32),
                   jax.ShapeDtypeStruct((nblk, r, S5_WIDTH), F32),
                   jax.ShapeDtypeStruct((nblk, r, S5_WIDTH), F32)],
        scratch_shapes=[pltpu.VMEM((rows, S5_WIDTH), F32), pltpu.VMEM((rows, S5_WIDTH), F32),
                        pltpu.VMEM((r, S5_WIDTH), F32), pltpu.VMEM((r, S5_WIDTH), F32)],
        compiler_params=_params("parallel", "arbitrary"),
        name="s5_branch",
    )(u_src, h0_re, h0_im, lw["lb_re"], lw["lb_im"], lw["bb_re"], lw["bb_im"], lw["c_re"], lw["c_im"],
      lw["s5_d"], lw["w_glu"], lw["b_glu"])


def _xattn_body(q_ref, k_ref, v_ref, o_ref, *, nb):
    for i in range(nb):
        q = q_ref[i].astype(BF16)
        k = k_ref[i].astype(BF16)
        v = v_ref[i].astype(BF16)
        outs = []
        for h in range(X_HEADS):
            sl = slice(h * X_HEAD_DIM, (h + 1) * X_HEAD_DIM)
            s = _dot_nt(q[:, sl], k[:, sl]) * X_SCALE
            p = jnp.exp(s - jnp.max(s, axis=-1, keepdims=True))
            p = p / jnp.sum(p, axis=-1, keepdims=True)
            outs.append(_dot(p.astype(BF16), v[:, sl]))
        o_ref[i] = jnp.concatenate(outs, axis=1)


def cross_attend(p2, mem_k, mem_v, bsz, t):
    n_mem = mem_k.shape[1]
    tq = min(512, t)
    nb = 8 if (t <= 8 and bsz % 8 == 0) else 1
    p3 = p2.reshape(bsz, t, P2_WIDTH)
    mem_map = lambda b, qi: (b, 0, 0)
    out = pl.pallas_call(
        functools.partial(_xattn_body, nb=nb),
        grid=(bsz // nb, t // tq),
        in_specs=[pl.BlockSpec((nb, tq, BRANCH_WIDTH), lambda b, qi: (b, qi, P2_Q_X)),
                  pl.BlockSpec((nb, n_mem, BRANCH_WIDTH), mem_map),
                  pl.BlockSpec((nb, n_mem, BRANCH_WIDTH), mem_map)],
        out_specs=pl.BlockSpec((nb, tq, BRANCH_WIDTH), lambda b, qi: (b, qi, 0)),
        out_shape=jax.ShapeDtypeStruct((bsz, t, BRANCH_WIDTH), F32),
        compiler_params=_params("parallel", "parallel"),
        name="cross_attend",
    )(p3, mem_k, mem_v)
    return out.reshape(bsz * t, BRANCH_WIDTH)


def _merge_body(x_ref, olat_ref, os5_ref, osb_ref, ox_ref, gmla_ref, gs5_ref, gsb_ref, gx_ref, gm_ref,
                wuv_ref, wbr_ref, wout_ref, gfin_ref, o_ref, *, final):
    o_mla = _dot(olat_ref[0].astype(BF16), wuv_ref[0])
    for h in range(1, MLA_HEADS):
        o_mla = o_mla + _dot(olat_ref[h].astype(BF16), wuv_ref[h])
    branches = ((o_mla, gmla_ref), (os5_ref[...], gs5_ref), (osb_ref[...], gsb_ref), (ox_ref[...], gx_ref))
    merged = None
    for n, (o, g_ref) in enumerate(branches):
        br = (o * jax.nn.silu(g_ref[...])).astype(BF16)
        proj = _dot(br, wbr_ref[n])
        term = jax.nn.sigmoid(gm_ref[:, n * D_MODEL:(n + 1) * D_MODEL]) * proj
        merged = term if merged is None else merged + term
    x = x_ref[...] + _dot(merged.astype(BF16), wout_ref[...])
    o_ref[...] = _rms(x, gfin_ref[...]) if final else x


def merge_out(x, o_lat, o_s5, o_sb, o_x, p2, lw, g_final, final, time_major_batch=None):
    m, d = x.shape
    tm = min(256, m)
    row = lambda i: (i, 0)
    const2 = lambda i: (0, 0)
    const3 = lambda i: (0, 0, 0)
    col = lambda c: (lambda i: (i, c))
    bw = pl.BlockSpec((tm, BRANCH_WIDTH), row)
    if time_major_batch is None:
        s5_spec = bw
    else:
        tiles_per_seq = m // time_major_batch // tm
        s5_spec = pl.BlockSpec((tm, BRANCH_WIDTH), lambda i: (i % tiles_per_seq, i // tiles_per_seq))
    return pl.pallas_call(
        functools.partial(_merge_body, final=final),
        grid=(m // tm,),
        in_specs=[pl.BlockSpec((tm, d), row),
                  pl.BlockSpec((MLA_HEADS, tm, KV_LORA), lambda i: (0, i, 0)),
                  s5_spec, bw, bw,
                  pl.BlockSpec((tm, BRANCH_WIDTH), col(P2_G_MLA)),
                  pl.BlockSpec((tm, BRANCH_WIDTH), col(P2_G_S5)),
                  pl.BlockSpec((tm, BRANCH_WIDTH), col(P2_G_SB)),
                  pl.BlockSpec((tm, BRANCH_WIDTH), col(P2_G_X)),
                  pl.BlockSpec((tm, P2_GMERGE_W), col(0)),
                  pl.BlockSpec(lw["wuv_pad"].shape, const3),
                  pl.BlockSpec(lw["w_branch"].shape, const3),
                  pl.BlockSpec(lw["w_out"].shape, const2),
                  pl.BlockSpec((1, d), const2)],
        out_specs=pl.BlockSpec((tm, d), row),
        out_shape=jax.ShapeDtypeStruct((m, d), F32),
        compiler_params=_params("parallel"),
        name="merge_out",
    )(x, o_lat, o_s5, o_sb, o_x, p2, p2, p2, p2, p2, lw["wuv_pad"], lw["w_branch"], lw["w_out"], g_final)


def _block_diag(blocks):
    g, r, c = blocks.shape
    eye = jnp.eye(g, dtype=blocks.dtype)
    return (eye[:, None, :, None] * blocks[:, :, None, :]).reshape(g * r, g * c)


def _layer_weights(l, g_norm, w_in, g_q_lat, g_kv_lat, w_uq, w_uk, w_uv, s5_a_re, s5_a_im, s5_log_dt,
                   s5_b_re, s5_b_im, s5_c_re, s5_c_im, s5_d, w_glu, b_glu, w_branch, w_out):
    w = w_in[l]
    o1 = Q_LORA + KV_LORA
    o2 = o1 + MLA_ROPE
    o3 = o2 + 9 * BRANCH_WIDTH
    rope_pad = jnp.zeros((D_MODEL, ROPE_PAD - MLA_ROPE), F32)
    w1 = jnp.concatenate([w[:, :o2], rope_pad], axis=1).astype(BF16)
    w2 = jnp.concatenate([w[:, o3:], w[:, o2:o3]], axis=1).astype(BF16)
    hd = MLA_NOPE + MLA_ROPE
    wq = w_uq[l].reshape(Q_LORA, MLA_HEADS, hd)
    wq_nope = wq[:, :, :MLA_NOPE].reshape(Q_LORA, MLA_HEADS * MLA_NOPE).astype(BF16)
    wq_pe = jnp.pad(wq[:, :, MLA_NOPE:], ((0, 0), (0, 0), (0, ROPE_PAD - MLA_ROPE))).reshape(
        Q_LORA, MLA_HEADS * ROPE_PAD).astype(BF16)
    wuk_bd = _block_diag(jnp.transpose(w_uk[l], (1, 2, 0))).astype(BF16)
    wuv = jnp.transpose(w_uv[l], (1, 0, 2))
    eye_h = jnp.eye(MLA_HEADS, dtype=F32)
    wuv_pad = (wuv[:, :, None, :] * eye_h[:, None, :, None]).reshape(MLA_HEADS, KV_LORA, BRANCH_WIDTH).astype(BF16)
    a_re, a_im = s5_a_re[l].astype(F32), s5_a_im[l].astype(F32)
    dt = jnp.exp(s5_log_dt[l].astype(F32))[:, None]
    mag = jnp.exp(a_re * dt)
    lb_re, lb_im = mag * jnp.cos(a_im * dt), mag * jnp.sin(a_im * dt)
    den = a_re * a_re + a_im * a_im
    n_re = lb_re - 1.0
    f_re = (n_re * a_re + lb_im * a_im) / den
    f_im = (lb_im * a_re - n_re * a_im) / den
    b_re, b_im = s5_b_re[l].astype(F32), s5_b_im[l].astype(F32)
    bb_re = f_re[..., None] * b_re - f_im[..., None] * b_im
    bb_im = f_re[..., None] * b_im + f_im[..., None] * b_re
    return dict(
        g_norm=g_norm[l].reshape(1, D_MODEL), w1=w1, w2=w2,
        g_q_lat=g_q_lat[l].reshape(1, Q_LORA), g_kv_lat=g_kv_lat[l].reshape(1, KV_LORA),
        wq_nope=wq_nope, wq_pe=wq_pe, wuk_bd=wuk_bd, wuv_pad=wuv_pad,
        lb_re=lb_re.reshape(1, S5_WIDTH), lb_im=lb_im.reshape(1, S5_WIDTH),
        bb_re=_block_diag(jnp.transpose(bb_re, (0, 2, 1))).astype(BF16),
        bb_im=_block_diag(jnp.transpose(bb_im, (0, 2, 1))).astype(BF16),
        c_re=_block_diag(jnp.transpose(s5_c_re[l].astype(F32), (0, 2, 1))).astype(BF16),
        c_im=_block_diag(jnp.transpose(s5_c_im[l].astype(F32), (0, 2, 1))).astype(BF16),
        s5_d=s5_d[l].reshape(1, BRANCH_WIDTH).astype(F32),
        w_glu=w_glu[l].astype(BF16), b_glu=b_glu[l].reshape(1, 2 * BRANCH_WIDTH),
        w_branch=w_branch[l].astype(BF16), w_out=w_out[l].astype(BF16))


def _rope_tables(pos):
    half = MLA_ROPE // 2
    freq = ROPE_BASE ** (-jnp.arange(half, dtype=F32) / half)
    ang = pos.astype(F32)[:, None] * freq
    cos, sin = jnp.cos(ang), jnp.sin(ang)
    pad = jnp.zeros((pos.shape[0], ROPE_PAD - MLA_ROPE), F32)
    return jnp.concatenate([cos, cos, pad], axis=1), jnp.concatenate([-sin, sin, pad], axis=1)


def _branches(x, lw, cos_t, sin_t, q_dtype):
    p2 = norm_matmul(x, lw["g_norm"], lw["w2"], tn=P2_WIDTH // 4)
    q_cat, ckv, kpe, k_cat = mla_prep(x, lw, cos_t, sin_t, q_dtype)
    return p2, q_cat, ckv, kpe, k_cat


def kernel(x_prompt, x_sample, mem_prompt, cache_mla_kv, cache_mla_pe, cache_sb_k, cache_sb_v, state_s5_re, state_s5_im, cache_mem_k, cache_mem_v, page_table, g_norm, w_in, g_q_lat, g_kv_lat, w_uq, w_uk, w_uv, s5_a_re, s5_a_im, s5_log_dt, s5_b_re, s5_b_im, s5_c_re, s5_c_im, s5_d, w_glu, b_glu, g_mem, w_mem_kv, w_branch, w_out, g_final):
    depth = w_in.shape[0]
    bp, tp, d = x_prompt.shape
    bs, ts, _ = x_sample.shape
    n_mem = mem_prompt.shape[1]
    n_pool = cache_mla_kv.shape[1]
    past_len = page_table.shape[1] * PAGE_SIZE
    mp, ms = bp * tp, bs * ts

    cos_p, sin_p = _rope_tables(jnp.arange(tp, dtype=jnp.int32))
    cos_p, sin_p = jnp.tile(cos_p, (bp, 1)), jnp.tile(sin_p, (bp, 1))
    cos_s, sin_s = _rope_tables(past_len + jnp.arange(ts, dtype=jnp.int32))
    cos_s, sin_s = jnp.tile(cos_s, (bs, 1)), jnp.tile(sin_s, (bs, 1))

    cache_kt = jnp.transpose(cache_sb_k, (0, 1, 3, 4, 2))
    cache_vt = jnp.transpose(cache_sb_v, (0, 1, 3, 4, 2))
    cache_pe_t = jnp.transpose(cache_mla_pe, (0, 1, 3, 2))
    page_table = page_table.astype(jnp.int32)
    g_fin = g_final.reshape(1, d)
    zeros_p = jnp.zeros((1, bp, S5_WIDTH), F32)

    r_s = 32 if bs % 32 == 0 else bs
    nblk_s = bs // r_s
    chunk_p = min(128, tp)

    xp = x_prompt.reshape(mp, d)
    xs = x_sample.reshape(ms, d)
    mem = mem_prompt.reshape(bp * n_mem, d)
    new_p = [[] for _ in range(8)]
    new_s = [[] for _ in range(6)]
    for l in range(depth):
        lw = _layer_weights(l, g_norm, w_in, g_q_lat, g_kv_lat, w_uq, w_uk, w_uv, s5_a_re, s5_a_im, s5_log_dt,
                            s5_b_re, s5_b_im, s5_c_re, s5_c_im, s5_d, w_glu, b_glu, w_branch, w_out)
        final = l == depth - 1
        mkv = norm_matmul(mem, g_mem[l], w_mem_kv[l].astype(BF16), tn=2 * BRANCH_WIDTH)
        mk = mkv[:, :BRANCH_WIDTH].reshape(bp, n_mem, BRANCH_WIDTH)
        mv = mkv[:, BRANCH_WIDTH:].reshape(bp, n_mem, BRANCH_WIDTH)
        p2, q_cat, ckv, kpe, k_cat = _branches(xp, lw, cos_p, sin_p, BF16)
        o_lat = mla_prompt(q_cat, k_cat, bp, tp)
        o_sb = sb_prompt(p2, bp, tp)
        u_p = p2[:, P2_U_S5 * BRANCH_WIDTH:(P2_U_S5 + 1) * BRANCH_WIDTH]
        u_p = u_p.reshape(bp, tp, BRANCH_WIDTH).transpose(1, 0, 2).reshape(mp, BRANCH_WIDTH)
        o_s5, h_re, h_im = s5_branch(u_p, 0, zeros_p, zeros_p, lw, 1, tp // chunk_p, chunk_p, bp,
                                     S5_WIDTH // 2)
        o_x = cross_attend(p2, mk, mv, bp, tp)
        xp = merge_out(xp, o_lat, o_s5.reshape(tp, bp * BRANCH_WIDTH), o_sb, o_x, p2, lw, g_fin, final,
                       time_major_batch=bp)
        st_p = (ckv.reshape(bp, tp, KV_LORA), kpe[:, :MLA_ROPE].reshape(bp, tp, MLA_ROPE),
                p2[:, P2_K_SB * BRANCH_WIDTH:(P2_K_SB + 1) * BRANCH_WIDTH].reshape(bp, tp, SB_HEADS, SB_HEAD_DIM),
                p2[:, P2_V_SB * BRANCH_WIDTH:(P2_V_SB + 1) * BRANCH_WIDTH].reshape(bp, tp, SB_HEADS, SB_HEAD_DIM),
                h_re.reshape(bp, S5_GROUPS, S5_STATE), h_im.reshape(bp, S5_GROUPS, S5_STATE),
                mk.reshape(bp, n_mem, X_HEADS, X_HEAD_DIM), mv.reshape(bp, n_mem, X_HEADS, X_HEAD_DIM))
        p2, q_cat, ckv, kpe, _ = _branches(xs, lw, cos_s, sin_s, F32)
        o_lat, o_sb = decode_attention(l, page_table, q_cat, ckv, kpe, p2, cache_mla_kv, cache_pe_t,
                                       cache_kt, cache_vt, bs, ts)
        u_s = p2[:, P2_U_S5 * BRANCH_WIDTH:(P2_U_S5 + 1) * BRANCH_WIDTH]
        u_s = u_s.reshape(nblk_s, r_s, ts, BRANCH_WIDTH).transpose(0, 2, 1, 3).reshape(ms, BRANCH_WIDTH)
        o_s5, h_re, h_im = s5_branch(u_s, 0, state_s5_re[l].reshape(nblk_s, r_s, S5_WIDTH),
                                     state_s5_im[l].reshape(nblk_s, r_s, S5_WIDTH), lw, nblk_s, 1, ts, r_s,
                                     min(512, S5_WIDTH))
        o_s5 = o_s5.reshape(nblk_s, ts, r_s, BRANCH_WIDTH).transpose(0, 2, 1, 3).reshape(ms, BRANCH_WIDTH)
        o_x = cross_attend(p2, cache_mem_k[l].reshape(bs, n_mem, BRANCH_WIDTH),
                           cache_mem_v[l].reshape(bs, n_mem, BRANCH_WIDTH), bs, ts)
        xs = merge_out(xs, o_lat, o_s5, o_sb, o_x, p2, lw, g_fin, final)
        st_s = (ckv.reshape(bs, ts, KV_LORA), kpe[:, :MLA_ROPE].reshape(bs, ts, MLA_ROPE),
                p2[:, P2_K_SB * BRANCH_WIDTH:(P2_K_SB + 1) * BRANCH_WIDTH].reshape(bs, ts, SB_HEADS, SB_HEAD_DIM),
                p2[:, P2_V_SB * BRANCH_WIDTH:(P2_V_SB + 1) * BRANCH_WIDTH].reshape(bs, ts, SB_HEADS, SB_HEAD_DIM),
                h_re.reshape(bs, S5_GROUPS, S5_STATE), h_im.reshape(bs, S5_GROUPS, S5_STATE))
        for lst, a in zip(new_p, st_p):
            lst.append(a)
        for lst, a in zip(new_s, st_s):
            lst.append(a)
    outs_p = [jnp.stack(a) for a in new_p]
    outs_s = [jnp.stack(a) for a in new_s]
    return (xp.reshape(bp, tp, d), xs.reshape(bs, ts, d), *outs_p, *outs_s)
```
